```python
import math
import jax, jax.numpy as jnp
from jax import lax
import numpy as np

D_MODEL = 1024
BATCH = 2
SEQ = 16384
DEPTH = 2

LRU_W = 256
LRU_BLOCKS = 4
LRU_BLOCK_W = LRU_W // LRU_BLOCKS
LRU_C = 8.0
CONV_W = 4
MLSTM_HEADS = 4
MLSTM_HEAD_DIM = 64
MLSTM_W = MLSTM_HEADS * MLSTM_HEAD_DIM
MLSTM_CHUNK = 64
SB_HEADS = 4
SB_HEAD_DIM = 64
SB_W = SB_HEADS * SB_HEAD_DIM
SB_BLOCK = 128
FF = ((8 * D_MODEL // 3 + 255) // 256) * 256
EPS = 1e-6

SPLIT_SIZES = (LRU_W, LRU_W, 2 * MLSTM_W, MLSTM_W, MLSTM_W, MLSTM_HEADS, MLSTM_HEADS,
               SB_W, SB_W, SB_W, D_MODEL, D_MODEL, D_MODEL)
IN_COLS = sum(SPLIT_SIZES)

kernel_name = "hybrid_rglru_mlstm_stickbreaking_block"


def rmsnorm(x, g):
    xf = x.astype(jnp.float32)
    y = xf * lax.rsqrt(jnp.mean(xf * xf, axis=-1, keepdims=True) + EPS)
    return (y * g.astype(jnp.float32)).astype(x.dtype)


def causal_dwconv(x, w):
    K, C = w.shape
    return lax.conv_general_dilated(
        x, w.astype(x.dtype)[:, None, :], window_strides=(1,), padding=[(K - 1, 0)],
        dimension_numbers=("NWC", "WIO", "NWC"), feature_group_count=C)


def to_heads(t, n_heads):
    B, S, W = t.shape
    return t.reshape(B, S, n_heads, W // n_heads).transpose(0, 2, 1, 3)


def from_heads(t):
    B, H, S, d = t.shape
    return t.transpose(0, 2, 1, 3).reshape(B, S, H * d)


def rg_lru(xb, wa, ba, wx, bx, lam):
    B, S, W = xb.shape
    xf = xb.astype(jnp.float32)
    xh = xf.reshape(B, S, LRU_BLOCKS, LRU_BLOCK_W)
    r = jax.nn.sigmoid(jnp.einsum('bsgi,gij->bsgj', xh, wa.astype(jnp.float32)).reshape(B, S, W)
                       + ba.astype(jnp.float32))
    i = jax.nn.sigmoid(jnp.einsum('bsgi,gij->bsgj', xh, wx.astype(jnp.float32)).reshape(B, S, W)
                       + bx.astype(jnp.float32))
    log_a = -LRU_C * r * jax.nn.softplus(-lam.astype(jnp.float32))
    a = jnp.exp(log_a)
    u = jnp.sqrt(-jnp.expm1(2.0 * log_a)) * (i * xf)

    def combine(left, right):
        a1, b1 = left
        a2, b2 = right
        return a1 * a2, a2 * b1 + b2

    _, h = lax.associative_scan(combine, (a, u), axis=1)
    return h.astype(xb.dtype)


def mlstm_chunkwise(q, k, v, i_pre, f_pre):
    B, H, S, dk = q.shape
    dv = v.shape[-1]
    L = MLSTM_CHUNK
    nc = S // L
    out_dtype = v.dtype
    q = q.astype(jnp.float32)
    k = k.astype(jnp.float32) * (dk ** -0.5)
    v = v.astype(jnp.float32)
    logi = i_pre.astype(jnp.float32)
    logf = jax.nn.log_sigmoid(f_pre.astype(jnp.float32))

    def chunks(t):
        return jnp.moveaxis(t.reshape((B, H, nc, L) + t.shape[3:]), 2, 0)

    tril = jnp.tril(jnp.ones((L, L), dtype=bool))

    def step(carry, inp):
        C, n, m = carry
        qc, kc, vc, ic, fc = inp
        b = jnp.cumsum(fc, axis=-1)
        D = jnp.where(tril, b[..., :, None] - b[..., None, :] + ic[..., None, :], -jnp.inf)
        m_inter = b + m[..., None]
        m_t = jnp.maximum(jnp.max(D, axis=-1), m_inter)
        P = jnp.exp(D - m_t[..., None]) * jnp.einsum('bhtk,bhsk->bhts', qc, kc)
        decay = jnp.exp(m_inter - m_t)
        num = (jnp.einsum('bhts,bhsv->bhtv', P, vc)
               + decay[..., None] * jnp.einsum('bhvk,bhtk->bhtv', C, qc))
        den = jnp.sum(P, axis=-1) + decay * jnp.einsum('bhk,bhtk->bht', n, qc)
        h = num / jnp.maximum(jnp.abs(den), jnp.exp(-m_t))[..., None]
        bL = b[..., -1]
        g = bL[..., None] - b + ic
        m_new = jnp.maximum(bL + m, jnp.max(g, axis=-1))
        wts = jnp.exp(g - m_new[..., None])
        carry_decay = jnp.exp(bL + m - m_new)
        C = carry_decay[..., None, None] * C + jnp.einsum('bhs,bhsv,bhsk->bhvk', wts, vc, kc)
        n = carry_decay[..., None] * n + jnp.einsum('bhs,bhsk->bhk', wts, kc)
        return (C, n, m_new), h

    init = (jnp.zeros((B, H, dv, dk), jnp.float32), jnp.zeros((B, H, dk), jnp.float32),
            jnp.zeros((B, H), jnp.float32))
    _, hs = lax.scan(step, init, (chunks(q), chunks(k), chunks(v), chunks(logi), chunks(logf)))
    return jnp.moveaxis(hs, 0, 2).reshape(B, H, S, dv).astype(out_dtype)


def stick_breaking_attention(q, k, v):
    B, H, S, d = q.shape
    nb = S // SB_BLOCK
    scale = 1.0 / math.sqrt(d)
    loc = jnp.arange(SB_BLOCK)
    diag_mask = loc[None, :] < loc[:, None]
    outs = []
    for bi in range(nb):
        nk = bi + 1
        lk_len = nk * SB_BLOCK
        qblk = q[:, :, bi * SB_BLOCK:(bi + 1) * SB_BLOCK]
        kk = k[:, :, :lk_len]
        vv = v[:, :, :lk_len]
        z = jnp.einsum('bhqd,bhsd->bhqs', qblk, kk).astype(jnp.float32) * scale
        causal = jnp.concatenate(
            [jnp.ones((SB_BLOCK, lk_len - SB_BLOCK), dtype=bool), diag_mask], axis=1)
        log_keep = jnp.where(causal, jax.nn.log_sigmoid(-z), 0.0)
        intra = lax.cumsum(log_keep.reshape(B, H, SB_BLOCK, nk, SB_BLOCK), axis=4, reverse=True)
        bsum = intra[..., 0]
        inter = lax.cumsum(bsum, axis=3, reverse=True) - bsum
        R = (intra + inter[..., None]).reshape(B, H, SB_BLOCK, lk_len)
        A = jnp.where(causal, jnp.exp(z + R), 0.0)
        outs.append(jnp.einsum('bhqs,bhsd->bhqd', A.astype(v.dtype), vv))
    return jnp.concatenate(outs, axis=2)


def setup_inputs(seed: int = 0) -> dict:
    key = jax.random.key(seed)
    ks = jax.random.split(key, 24)
    f32 = jnp.float32
    D = D_MODEL

    def nrm(k, shape, fan_in):
        return jax.random.normal(k, shape, f32) * (fan_in ** -0.5)

    a0 = jax.random.uniform(ks[8], (DEPTH, LRU_W), f32, 0.9, 0.999)
    s0 = a0 ** (1.0 / LRU_C)
    lru_lambda = jnp.log(s0) - jnp.log1p(-s0)
    fg_b = (jnp.linspace(3.0, 6.0, MLSTM_HEADS, dtype=f32)[None, :]
            + 0.1 * jax.random.normal(ks[11], (DEPTH, MLSTM_HEADS), f32))
    return {
        "x": jax.random.normal(ks[0], (BATCH, SEQ, D), f32),
        "norm_mix_g": 1.0 + 0.02 * jax.random.normal(ks[1], (DEPTH, D), f32),
        "w_in": nrm(ks[2], (DEPTH, D, IN_COLS), D),
        "conv_lru_w": nrm(ks[3], (DEPTH, CONV_W, LRU_W), CONV_W),
        "lru_wa": nrm(ks[4], (DEPTH, LRU_BLOCKS, LRU_BLOCK_W, LRU_BLOCK_W), LRU_BLOCK_W),
        "lru_ba": 0.02 * jax.random.normal(ks[5], (DEPTH, LRU_W), f32),
        "lru_wx": nrm(ks[6], (DEPTH, LRU_BLOCKS, LRU_BLOCK_W, LRU_BLOCK_W), LRU_BLOCK_W),
        "lru_bx": 0.02 * jax.random.normal(ks[7], (DEPTH, LRU_W), f32),
        "lru_lambda": lru_lambda,
        "conv_mlstm_w": nrm(ks[9], (DEPTH, CONV_W, 2 * MLSTM_W), CONV_W),
        "mlstm_ig_b": 0.1 * jax.random.normal(ks[10], (DEPTH, MLSTM_HEADS), f32),
        "mlstm_fg_b": fg_b,
        "w_out_lru": nrm(ks[12], (DEPTH, LRU_W, D), LRU_W),
        "w_out_mlstm": nrm(ks[13], (DEPTH, MLSTM_W, D), MLSTM_W),
        "w_out_sb": nrm(ks[14], (DEPTH, SB_W, D), SB_W),
        "w_o": nrm(ks[15], (DEPTH, D, D), D),
        "norm_ffn_g": 1.0 + 0.02 * jax.random.normal(ks[16], (DEPTH, D), f32),
        "w_ffn_in": nrm(ks[17], (DEPTH, D, 2 * FF), D),
        "w_ffn_out": nrm(ks[18], (DEPTH, FF, D), FF),
        "final_norm_g": 1.0 + 0.02 * jax.random.normal(ks[19], (D,), f32),
    }


def reference(x, norm_mix_g, w_in, conv_lru_w, lru_wa, lru_ba, lru_wx, lru_bx, lru_lambda,
              conv_mlstm_w, mlstm_ig_b, mlstm_fg_b, w_out_lru, w_out_mlstm, w_out_sb, w_o,
              norm_ffn_g, w_ffn_in, w_ffn_out, final_norm_g):
    split_points = np.cumsum(SPLIT_SIZES)[:-1].tolist()
    for l in range(DEPTH):
        h = rmsnorm(x, norm_mix_g[l])
        proj = h @ w_in[l]
        (lru_x, lru_gate, m_qk, m_v, m_o, m_i, m_f,
         s_q, s_k, s_v, g_a, g_b, g_c) = jnp.split(proj, split_points, axis=-1)

        xa = causal_dwconv(lru_x, conv_lru_w[l])
        ya = rg_lru(xa, lru_wa[l], lru_ba[l], lru_wx[l], lru_bx[l], lru_lambda[l])
        ya = (ya * jax.nn.gelu(lru_gate)) @ w_out_lru[l]

        qk = jax.nn.silu(causal_dwconv(m_qk, conv_mlstm_w[l]))
        mq, mk = jnp.split(qk, 2, axis=-1)
        i_pre = (m_i + mlstm_ig_b[l]).transpose(0, 2, 1)
        f_pre = (m_f + mlstm_fg_b[l]).transpose(0, 2, 1)
        hb = mlstm_chunkwise(to_heads(mq, MLSTM_HEADS), to_heads(mk, MLSTM_HEADS),
                             to_heads(m_v, MLSTM_HEADS), i_pre, f_pre)
        yb = (jax.nn.sigmoid(m_o) * from_heads(hb)) @ w_out_mlstm[l]

        hc = stick_breaking_attention(to_heads(s_q, SB_HEADS), to_heads(s_k, SB_HEADS),
                                      to_heads(s_v, SB_HEADS))
        yc = from_heads(hc) @ w_out_sb[l]

        merged = (jax.nn.sigmoid(g_a) * ya + jax.nn.sigmoid(g_b) * yb
                  + jax.nn.sigmoid(g_c) * yc)
        x = x + merged @ w_o[l]

        hf = rmsnorm(x, norm_ffn_g[l])
        gate, up = jnp.split(hf @ w_ffn_in[l], 2, axis=-1)
        x = x + (jax.nn.silu(gate) * up) @ w_ffn_out[l]
    return rmsnorm(x, final_norm_g)
```

```python
import functools
import math

import jax
import jax.numpy as jnp
from jax import lax
from jax.experimental import pallas as pl
from jax.experimental.pallas import tpu as pltpu

F32 = jnp.float32
BF16 = jnp.bfloat16

LANES = 128
SUBLANES = 8
VMEM_LIMIT_BYTES = 56 * 1024 * 1024

EPS = 1e-6
LRU_C = 8.0
CONV_W = 4
HEAD_DIM = 64
MLSTM_CHUNK = 64
SB_BLOCK = 128
SB_LOG_WEIGHT_FLOOR = -120.0

TOKEN_TILE = 512
SCAN_TILE = 512
MLSTM_TILE = 256
SB_QTILE = 512
DOT_COLS = 512


def _params(*semantics):
    return pltpu.CompilerParams(dimension_semantics=semantics, vmem_limit_bytes=VMEM_LIMIT_BYTES)


def _full(shape):
    return pl.BlockSpec(shape, lambda *_: (0,) * len(shape))


def _rmsnorm(x, g):
    return x * lax.rsqrt(jnp.mean(x * x, axis=-1, keepdims=True) + EPS) * g


def _softplus(x):
    return jnp.maximum(x, 0.0) + jnp.log1p(jnp.exp(-jnp.abs(x)))


def _sigmoid(x):
    return 1.0 / (1.0 + jnp.exp(-x))


def _silu(x):
    return x * _sigmoid(x)


def _gelu_tanh(x):
    return 0.5 * x * (1.0 + jnp.tanh(math.sqrt(2.0 / math.pi) * (x + 0.044715 * (x * x * x))))


def _one_minus_exp(y):
    acc = 1.0 + y * (1.0 / 12.0)
    for n in range(11, 1, -1):
        acc = 1.0 + (y * (1.0 / n)) * acc
    return jnp.where(y > -0.5, -(y * acc), 1.0 - jnp.exp(y))


def _dot(a, b):
    return jnp.dot(a, b, preferred_element_type=F32)


def _dot_nt(a, b):
    return lax.dot_general(a, b, (((1,), (1,)), ((), ())), preferred_element_type=F32)


def _dot_tn(a, b):
    return lax.dot_general(a, b, (((0,), (0,)), ((), ())), preferred_element_type=F32)


def _dot_exact_f32(a, b):
    return jnp.dot(a, b, preferred_element_type=F32, precision=lax.Precision.HIGHEST)


def _causal_conv(x, tail, w):
    rows8 = lax.broadcasted_iota(jnp.int32, (SUBLANES, x.shape[1]), 0)
    out = x * w[CONV_W - 1:CONV_W, :]
    for back in range(1, CONV_W):
        shifted = pltpu.roll(x, back, 0)
        head = jnp.where(rows8 < back, pltpu.roll(tail, back, 0), shifted[:SUBLANES])
        shifted = jnp.concatenate([head, shifted[SUBLANES:]], axis=0)
        out = out + shifted * w[CONV_W - 1 - back:CONV_W - back, :]
    return out


def _inproj_kernel(x_ref, g_ref, *refs):
    n = len(refs) // 2
    w_refs, o_refs = refs[:n], refs[n:]
    h = _rmsnorm(x_ref[...], g_ref[...]).astype(BF16)
    for w_ref, o_ref in zip(w_refs, o_refs):
        cols = w_ref.shape[1]
        for c0 in range(0, cols, DOT_COLS):
            c1 = min(c0 + DOT_COLS, cols)
            o_ref[:, c0:c1] = _dot(h, w_ref[:, c0:c1]).astype(o_ref.dtype)


def _inproj(x, g, weights, out_dtypes):
    T, D = x.shape
    tm = TOKEN_TILE
    in_specs = [pl.BlockSpec((tm, D), lambda i: (i, 0)), _full((1, D))]
    in_specs += [_full(w.shape) for w in weights]
    out_specs = [pl.BlockSpec((tm, w.shape[1]), lambda i: (i, 0)) for w in weights]
    out_shape = [jax.ShapeDtypeStruct((T, w.shape[1]), dt) for w, dt in zip(weights, out_dtypes)]
    return pl.pallas_call(
        _inproj_kernel, grid=(T // tm,), in_specs=in_specs, out_specs=out_specs,
        out_shape=out_shape, compiler_params=_params("parallel"), name="inproj",
    )(x, g, *weights)


def _lru_kernel(xg_ref, cw_ref, wa_ref, wx_ref, ba_ref, bx_ref, lam_ref, o_ref, tail_ref, h_ref):
    W = o_ref.shape[1]
    ts = o_ref.shape[0]

    @pl.when(pl.program_id(1) == 0)
    def _():
        tail_ref[...] = jnp.zeros_like(tail_ref)
        h_ref[...] = jnp.zeros_like(h_ref)

    x = xg_ref[:, :W]
    gate = xg_ref[:, W:]
    xa = _causal_conv(x, tail_ref[...], cw_ref[...])
    tail_ref[...] = x[ts - SUBLANES:, :]

    xb = xa.astype(BF16)
    r = _sigmoid(_dot(xb, wa_ref[...]) + ba_ref[...])
    i = _sigmoid(_dot(xb, wx_ref[...]) + bx_ref[...])
    log_a = (-LRU_C) * r * _softplus(-lam_ref[...])
    a = jnp.exp(log_a)
    u = jnp.sqrt(_one_minus_exp(2.0 * log_a)) * (i * xa)

    rows = lax.broadcasted_iota(jnp.int32, (ts, W), 0)
    span = 1
    while span < ts:
        keep = rows >= span
        a_prev = jnp.where(keep, pltpu.roll(a, span, 0), 1.0)
        u_prev = jnp.where(keep, pltpu.roll(u, span, 0), 0.0)
        u = a * u_prev + u
        a = a * a_prev
        span *= 2
    h = u + a * h_ref[SUBLANES - 1:SUBLANES, :]
    h_ref[...] = h[ts - SUBLANES:, :]
    o_ref[...] = (h * _gelu_tanh(gate)).astype(o_ref.dtype)


def _lru(xg, conv_w, wa, wx, ba, bx, lam, batch):
    T, W2 = xg.shape
    W = W2 // 2
    ts = SCAN_TILE
    nt = T // batch // ts
    return pl.pallas_call(
        _lru_kernel, grid=(batch, nt),
        in_specs=[pl.BlockSpec((ts, W2), lambda b, t: (b * nt + t, 0)),
                  _full(conv_w.shape), _full(wa.shape), _full(wx.shape),
                  _full(ba.shape), _full(bx.shape), _full(lam.shape)],
        out_specs=pl.BlockSpec((ts, W), lambda b, t: (b * nt + t, 0)),
        out_shape=jax.ShapeDtypeStruct((T, W), BF16),
        scratch_shapes=[pltpu.VMEM((SUBLANES, W), F32), pltpu.VMEM((SUBLANES, W), F32)],
        compiler_params=_params("parallel", "arbitrary"), name="rglru",
    )(xg, conv_w, wa, wx, ba, bx, lam)


def _mlstm_kernel(qk_ref, vo_ref, ifc_ref, ifr_ref, cw_ref, bc_ref, br_ref, o_ref,
                  tail_ref, ct_ref, n_ref, m_ref):
    ts, W = o_ref.shape
    L = MLSTM_CHUNK
    heads = W // HEAD_DIM
    pairs = W // LANES

    @pl.when(pl.program_id(1) == 0)
    def _():
        tail_ref[...] = jnp.zeros_like(tail_ref)
        ct_ref[...] = jnp.zeros_like(ct_ref)
        n_ref[...] = jnp.zeros_like(n_ref)
        m_ref[...] = jnp.zeros_like(m_ref)

    x = qk_ref[...]
    qk = _silu(_causal_conv(x, tail_ref[...], cw_ref[...]))
    tail_ref[...] = x[ts - SUBLANES:, :]
    q_all = qk[:, :W]
    k_all = qk[:, W:] * (HEAD_DIM ** -0.5)
    v_all = vo_ref[:, :W]
    o_gate = _sigmoid(vo_ref[:, W:].astype(F32))

    pre_c = ifc_ref[...] + bc_ref[...]
    logf_c = -_softplus(-pre_c)
    r_i = lax.broadcasted_iota(jnp.int32, (ts, ts), 0)
    c_i = lax.broadcasted_iota(jnp.int32, (ts, ts), 1)
    chunk_bits = L.bit_length() - 1
    same_chunk_lower = jnp.logical_and(
        lax.shift_right_logical(r_i, chunk_bits) == lax.shift_right_logical(c_i, chunk_bits),
        c_i <= r_i)
    b_col = _dot_exact_f32(jnp.where(same_chunk_lower, 1.0, 0.0), logf_c)

    rl = lax.broadcasted_iota(jnp.int32, (L, L), 0)
    cl = lax.broadcasted_iota(jnp.int32, (L, L), 1)
    tril = cl <= rl
    upper_ones = jnp.where(rl <= cl, 1.0, 0.0)
    lane = lax.broadcasted_iota(jnp.int32, (1, LANES), 1)
    head_mask = [lane < HEAD_DIM, lane >= HEAD_DIM]

    m_state = [m_ref[h:h + 1, 0:1] for h in range(heads)]
    ct_state = [ct_ref[p] for p in range(pairs)]
    n_state = [n_ref[p] for p in range(pairs)]

    for c in range(ts // L):
        r0 = c * L
        pre_r = ifr_ref[0, c] + br_ref[...]
        b_row = _dot_exact_f32(-_softplus(-pre_r), upper_ones)
        for p in range(pairs):
            lanes = slice(p * LANES, (p + 1) * LANES)
            q_p = q_all[r0:r0 + L, lanes]
            k_p = k_all[r0:r0 + L, lanes]
            v_p = v_all[r0:r0 + L, lanes]
            ct_bf = ct_state[p].astype(BF16)
            h_pair = jnp.zeros((L, LANES), F32)
            ct_add = jnp.zeros((LANES, LANES), F32)
            n_add = jnp.zeros((1, LANES), F32)
            decay_lane = jnp.zeros((1, LANES), F32)
            for hh in range(2):
                h = 2 * p + hh
                mask = head_mask[hh]
                q_h = jnp.where(mask, q_p, 0.0)
                k_h = jnp.where(mask, k_p, 0.0)
                v_h = jnp.where(mask, v_p, jnp.zeros_like(v_p))
                q_hb = q_h.astype(BF16)
                b_c = b_col[r0:r0 + L, heads + h:heads + h + 1]
                i_c = pre_c[r0:r0 + L, h:h + 1]
                b_r = b_row[heads + h:heads + h + 1, :]
                i_r = pre_r[h:h + 1, :]
                m_prev = m_state[h]

                d = jnp.where(tril, b_c - b_r + i_r, -jnp.inf)
                m_inter = b_c + m_prev
                m_t = jnp.maximum(jnp.max(d, axis=-1, keepdims=True), m_inter)
                pw = jnp.exp(d - m_t) * _dot_nt(q_hb, k_h.astype(BF16))
                decay = jnp.exp(m_inter - m_t)
                num = _dot(pw.astype(BF16), v_h) + decay * _dot(q_hb, ct_bf)
                den = (jnp.sum(pw, axis=-1, keepdims=True)
                       + decay * jnp.sum(q_h * n_state[p], axis=-1, keepdims=True))
                h_pair = h_pair + num / jnp.maximum(jnp.abs(den), jnp.exp(-m_t))

                b_last = b_c[L - 1:L, :]
                g = b_last - b_c + i_c
                m_new = jnp.maximum(b_last + m_prev, jnp.max(g, axis=0, keepdims=True))
                kw = k_h * jnp.exp(g - m_new)
                ct_add = ct_add + _dot_tn(kw.astype(BF16), v_h)
                n_add = n_add + jnp.sum(kw, axis=0, keepdims=True)
                decay_lane = decay_lane + jnp.where(mask, jnp.exp(b_last + m_prev - m_new), 0.0)
                m_state[h] = m_new
            ct_state[p] = decay_lane * ct_state[p] + ct_add
            n_state[p] = decay_lane * n_state[p] + n_add
            o_ref[r0:r0 + L, lanes] = (o_gate[r0:r0 + L, lanes] * h_pair).astype(o_ref.dtype)

    for p in range(pairs):
        ct_ref[p] = ct_state[p]
        n_ref[p] = n_state[p]
    for h in range(heads):
        m_ref[h:h + 1, :] = jnp.broadcast_to(m_state[h], (1, LANES))


def _mlstm(qk, vo, if_cols, if_rows, conv_w, bias_col, bias_row, batch):
    T, W2 = qk.shape
    W = W2 // 2
    ts = MLSTM_TILE
    nt = T // batch // ts
    nch = ts // MLSTM_CHUNK
    pairs = W // LANES
    tok = lambda b, t: (b * nt + t, 0)
    return pl.pallas_call(
        _mlstm_kernel, grid=(batch, nt),
        in_specs=[pl.BlockSpec((ts, W2), tok), pl.BlockSpec((ts, W2), tok),
                  pl.BlockSpec((ts, LANES), tok),
                  pl.BlockSpec((1, nch, SUBLANES, MLSTM_CHUNK), lambda b, t: (b, t, 0, 0)),
                  _full(conv_w.shape), _full(bias_col.shape), _full(bias_row.shape)],
        out_specs=pl.BlockSpec((ts, W), tok),
        out_shape=jax.ShapeDtypeStruct((T, W), BF16),
        scratch_shapes=[pltpu.VMEM((SUBLANES, W2), F32),
                        pltpu.VMEM((pairs, LANES, LANES), F32),
                        pltpu.VMEM((pairs, 1, LANES), F32),
                        pltpu.VMEM((SUBLANES, LANES), F32)],
        compiler_params=_params("parallel", "arbitrary"), name="mlstm",
    )(qk, vo, if_cols, if_rows, conv_w, bias_col, bias_row)


def _sb_kernel(q_ref, k_ref, v_ref, o_ref):
    tq = q_ref.shape[0]
    blk = SB_BLOCK
    scale = 1.0 / math.sqrt(HEAD_DIM)
    lane = lax.broadcasted_iota(jnp.int32, (1, LANES), 1)
    head_mask = [lane < HEAD_DIM, lane >= HEAD_DIM]
    rq = lax.broadcasted_iota(jnp.int32, (blk, blk), 0)
    ck = lax.broadcasted_iota(jnp.int32, (blk, blk), 1)
    strictly_causal = ck < rq
    suffix_ones = jnp.where(rq >= ck, 1.0, 0.0).astype(BF16)

    def block_terms(q_h, k, carry, causal):
        z = _dot_nt(q_h, k) * scale
        log_keep = -_softplus(z)
        if causal is not None:
            log_keep = jnp.where(causal, log_keep, 0.0)
        hi = log_keep.astype(BF16)
        lo = (log_keep - hi.astype(F32)).astype(BF16)
        suffix = _dot(hi, suffix_ones) + _dot(lo, suffix_ones)
        weights = jnp.exp(z + suffix + carry)
        if causal is not None:
            weights = jnp.where(causal, weights, 0.0)
        return weights.astype(BF16), carry + suffix[:, 0:1]

    for j in range(tq // blk):
        qb = pl.program_id(2) * (tq // blk) + j
        q = q_ref[j * blk:(j + 1) * blk, :]
        q_heads = [jnp.where(m, q, jnp.zeros_like(q)) for m in head_mask]

        def visit(kb, carries, acc, causal):
            rows = pl.ds(pl.multiple_of(kb * blk, blk), blk)
            k = k_ref[rows, :]
            v = v_ref[rows, :]
            new_carries = []
            for hh in range(2):
                w, c = block_terms(q_heads[hh], k, carries[hh], causal)
                acc = acc + _dot(w, jnp.where(head_mask[hh], v, jnp.zeros_like(v)))
                new_carries.append(c)
            return tuple(new_carries), acc

        zero_carry = jnp.zeros((blk, 1), F32)
        carries, acc = visit(qb, (zero_carry, zero_carry), jnp.zeros((blk, LANES), F32),
                             strictly_causal)

        def cond(state):
            kb, carries, _ = state
            live = jnp.max(jnp.maximum(carries[0], carries[1])) > SB_LOG_WEIGHT_FLOOR
            return jnp.logical_and(kb >= 0, live)

        def body(state):
            kb, carries, acc = state
            carries, acc = visit(kb, carries, acc, None)
            return kb - 1, carries, acc

        _, _, acc = lax.while_loop(cond, body, (qb - 1, carries, acc))
        o_ref[j * blk:(j + 1) * blk, :] = acc.astype(o_ref.dtype)


def _stick_breaking(qkv, batch):
    T, W3 = qkv.shape
    W = W3 // 3
    S = T // batch
    pairs = W // LANES
    tq = SB_QTILE
    nq = S // tq
    return pl.pallas_call(
        _sb_kernel, grid=(batch, pairs, nq),
        in_specs=[pl.BlockSpec((tq, LANES), lambda b, p, i: (b * nq + i, p)),
                  pl.BlockSpec((S, LANES), lambda b, p, i: (b, pairs + p)),
                  pl.BlockSpec((S, LANES), lambda b, p, i: (b, 2 * pairs + p))],
        out_specs=pl.BlockSpec((tq, LANES), lambda b, p, i: (b * nq + i, p)),
        out_shape=jax.ShapeDtypeStruct((T, W), BF16),
        compiler_params=_params("parallel", "parallel", "arbitrary"), name="stickbreak",
    )(qkv, qkv, qkv)


def _merge_ffn_kernel(x_ref, ya_ref, yb_ref, yc_ref, gates_ref, wa_ref, wb_ref, wc_ref, wo_ref,
                      gf_ref, wi_ref, wout_ref, gfin_ref, o_ref, act_ref, *, final_norm):
    D = x_ref.shape[1]
    FF = wout_ref.shape[0]
    merged = jnp.zeros(x_ref.shape, F32)
    for idx, (y_ref, w_ref) in enumerate(((ya_ref, wa_ref), (yb_ref, wb_ref), (yc_ref, wc_ref))):
        gate = _sigmoid(gates_ref[:, idx * D:(idx + 1) * D].astype(F32))
        merged = merged + gate * _dot(y_ref[...], w_ref[...])
    x1 = x_ref[...] + _dot(merged.astype(BF16), wo_ref[...])
    hf = _rmsnorm(x1, gf_ref[...]).astype(BF16)
    c0 = 0
    while c0 < FF:
        c1 = min(c0 + DOT_COLS, FF)
        gate = _dot(hf, wi_ref[:, c0:c1])
        up = _dot(hf, wi_ref[:, FF + c0:FF + c1])
        act_ref[:, c0:c1] = (_silu(gate) * up).astype(BF16)
        c0 = c1
    x2 = x1 + _dot(act_ref[...], wout_ref[...])
    if final_norm:
        x2 = _rmsnorm(x2, gfin_ref[...])
    o_ref[...] = x2


def _merge_ffn(x, ya, yb, yc, gates, wa, wb, wc, wo, gf, wi, wout, gfin, final_norm):
    T, D = x.shape
    tm = TOKEN_TILE
    tok = lambda i: (i, 0)
    return pl.pallas_call(
        functools.partial(_merge_ffn_kernel, final_norm=final_norm), grid=(T // tm,),
        in_specs=[pl.BlockSpec((tm, D), tok), pl.BlockSpec((tm, ya.shape[1]), tok),
                  pl.BlockSpec((tm, yb.shape[1]), tok), pl.BlockSpec((tm, yc.shape[1]), tok),
                  pl.BlockSpec((tm, gates.shape[1]), tok),
                  _full(wa.shape), _full(wb.shape), _full(wc.shape), _full(wo.shape),
                  _full(gf.shape), _full(wi.shape), _full(wout.shape), _full(gfin.shape)],
        out_specs=pl.BlockSpec((tm, D), tok),
        out_shape=jax.ShapeDtypeStruct((T, D), F32),
        scratch_shapes=[pltpu.VMEM((tm, wout.shape[0]), BF16)],
        compiler_params=_params("parallel"), name="merge_ffn",
    )(x, ya, yb, yc, gates, wa, wb, wc, wo, gf, wi, wout, gfin)


def _block_diag(w):
    G, n, _ = w.shape
    eye = jnp.eye(G, dtype=w.dtype)
    return (eye[:, None, :, None] * w[:, :, None, :]).reshape(G * n, G * n)


def kernel(x, norm_mix_g, w_in, conv_lru_w, lru_wa, lru_ba, lru_wx, lru_bx, lru_lambda, conv_mlstm_w, mlstm_ig_b, mlstm_fg_b, w_out_lru, w_out_mlstm, w_out_sb, w_o, norm_ffn_g, w_ffn_in, w_ffn_out, final_norm_g):
    batch, seq, D = x.shape
    depth = w_in.shape[0]
    lru_w = conv_lru_w.shape[-1]
    mlstm_w = conv_mlstm_w.shape[-1] // 2
    heads = mlstm_ig_b.shape[-1]
    sb_w = w_out_sb.shape[1]
    T = batch * seq

    edges = [0]
    for width in (2 * lru_w, 2 * mlstm_w, 2 * mlstm_w, 2 * heads, 3 * sb_w, 3 * D):
        edges.append(edges[-1] + width)
    assert edges[-1] == w_in.shape[-1]

    xt = x.reshape(T, D)
    row = lambda v: v.reshape(1, -1).astype(F32)
    for l in range(depth):
        w = w_in[l]
        w_lru, w_mqk, w_mvo, w_if, w_sqkv, w_gates = (
            w[:, edges[i]:edges[i + 1]].astype(BF16) for i in range(6))
        w_if = jnp.pad(w_if, ((0, 0), (0, LANES - 2 * heads)))
        lru_xg, m_qk, m_vo, m_if, s_qkv, gates = _inproj(
            xt, row(norm_mix_g[l]), (w_lru, w_mqk, w_mvo, w_if, w_sqkv, w_gates),
            (F32, F32, BF16, F32, BF16, BF16))

        ya = _lru(lru_xg, conv_lru_w[l].astype(F32), _block_diag(lru_wa[l]).astype(BF16),
                  _block_diag(lru_wx[l]).astype(BF16), row(lru_ba[l]), row(lru_bx[l]),
                  row(lru_lambda[l]), batch)

        gate_bias = jnp.concatenate([mlstm_ig_b[l], mlstm_fg_b[l]]).astype(F32)
        bias_col = jnp.pad(gate_bias, (0, LANES - 2 * heads)).reshape(1, LANES)
        bias_row = gate_bias.reshape(2 * heads, 1)
        if_rows = (m_if[:, :2 * heads].reshape(batch, seq // MLSTM_CHUNK, MLSTM_CHUNK, 2 * heads)
                   .transpose(0, 1, 3, 2))
        yb = _mlstm(m_qk, m_vo, m_if, if_rows, conv_mlstm_w[l].astype(F32), bias_col, bias_row,
                    batch)

        yc = _stick_breaking(s_qkv, batch)

        xt = _merge_ffn(xt, ya, yb, yc, gates, w_out_lru[l].astype(BF16),
                        w_out_mlstm[l].astype(BF16), w_out_sb[l].astype(BF16),
                        w_o[l].astype(BF16), row(norm_ffn_g[l]), w_ffn_in[l].astype(BF16),
                        w_ffn_out[l].astype(BF16), row(final_norm_g), final_norm=(l == depth - 1))
    return xt.reshape(batch, seq, D)
```

```python
import functools
import math

import jax
import jax.numpy as jnp
import numpy as np
from jax import lax
from jax.experimental import pallas as pl
from jax.experimental.pallas import tpu as pltpu

F32 = jnp.float32
BF16 = jnp.bfloat16

LANES = 128
SUBLANES = 8
VMEM_LIMIT_BYTES = 56 * 1024 * 1024

EPS = 1e-6
LRU_C = 8.0
CONV_W = 4
HEAD_DIM = 64
MLSTM_CHUNK = 64
SB_QSUB = 64
SB_WINDOW = 256
SB_TAIL = 64
SB_LOG_WEIGHT_FLOOR = -110.0

TOKEN_TILE = 512
SCAN_TILE = 512
MLSTM_TILE = 256
SB_QTILE = 512
DOT_COLS = 512


def _params(*semantics):
    return pltpu.CompilerParams(dimension_semantics=semantics, vmem_limit_bytes=VMEM_LIMIT_BYTES)


def _full(shape):
    return pl.BlockSpec(shape, lambda *_: (0,) * len(shape))


def _rmsnorm(x, g):
    return x * lax.rsqrt(jnp.mean(x * x, axis=-1, keepdims=True) + EPS) * g


def _softplus(x):
    return jnp.maximum(x, 0.0) + jnp.log1p(jnp.exp(-jnp.abs(x)))


def _softplus_fast(x):
    return jnp.maximum(x, 0.0) + jnp.log(1.0 + jnp.exp(-jnp.abs(x)))


def _sigmoid(x):
    return 1.0 / (1.0 + jnp.exp(-x))


def _silu(x):
    return x * _sigmoid(x)


def _gelu_tanh(x):
    return 0.5 * x * (1.0 + jnp.tanh(math.sqrt(2.0 / math.pi) * (x + 0.044715 * (x * x * x))))


def _one_minus_exp(y):
    poly = jnp.full_like(y, 1.0 / math.factorial(7))
    for n in range(6, 0, -1):
        poly = poly * y + 1.0 / math.factorial(n)
    return jnp.where(y > -0.125, -(y * poly), 1.0 - jnp.exp(y))


def _dot(a, b):
    return jnp.dot(a, b, preferred_element_type=F32)


def _dot_nt(a, b):
    return lax.dot_general(a, b, (((1,), (1,)), ((), ())), preferred_element_type=F32)


def _dot_tn(a, b):
    return lax.dot_general(a, b, (((0,), (0,)), ((), ())), preferred_element_type=F32)


def _dot_exact_f32(a, b):
    return jnp.dot(a, b, preferred_element_type=F32, precision=lax.Precision.HIGHEST)


def _causal_conv(pad_ref, x, w):
    ts = x.shape[0]
    pad_ref[0:SUBLANES, :] = pad_ref[ts:ts + SUBLANES, :]
    pad_ref[SUBLANES:, :] = x
    out = x * w[CONV_W - 1:CONV_W, :]
    for back in range(1, CONV_W):
        out = out + (pad_ref[SUBLANES - back:SUBLANES - back + ts, :]
                     * w[CONV_W - 1 - back:CONV_W - back, :])
    return out


def _inproj_kernel(x_ref, g_ref, *refs):
    n = len(refs) // 2
    w_refs, o_refs = refs[:n], refs[n:]
    h = _rmsnorm(x_ref[...], g_ref[...]).astype(BF16)
    for w_ref, o_ref in zip(w_refs, o_refs):
        cols = w_ref.shape[1]
        for c0 in range(0, cols, DOT_COLS):
            c1 = min(c0 + DOT_COLS, cols)
            o_ref[:, c0:c1] = _dot(h, w_ref[:, c0:c1]).astype(o_ref.dtype)


def _inproj(x, g, weights, out_dtypes):
    T, D = x.shape
    tm = TOKEN_TILE
    in_specs = [pl.BlockSpec((tm, D), lambda i: (i, 0)), _full((1, D))]
    in_specs += [_full(w.shape) for w in weights]
    out_specs = [pl.BlockSpec((tm, w.shape[1]), lambda i: (i, 0)) for w in weights]
    out_shape = [jax.ShapeDtypeStruct((T, w.shape[1]), dt) for w, dt in zip(weights, out_dtypes)]
    return pl.pallas_call(
        _inproj_kernel, grid=(T // tm,), in_specs=in_specs, out_specs=out_specs,
        out_shape=out_shape, compiler_params=_params("parallel"), name="inproj",
    )(x, g, *weights)


def _lru_kernel(xg_ref, cw_ref, wa_ref, wx_ref, ba_ref, bx_ref, lam_ref, o_ref, pad_ref, h_ref):
    W = o_ref.shape[1]
    ts = o_ref.shape[0]

    @pl.when(pl.program_id(1) == 0)
    def _():
        pad_ref[...] = jnp.zeros_like(pad_ref)
        h_ref[...] = jnp.zeros_like(h_ref)

    gate = xg_ref[:, W:]
    xa = _causal_conv(pad_ref, xg_ref[:, :W], cw_ref[...])

    xb = xa.astype(BF16)
    r = _sigmoid(_dot(xb, wa_ref[...]) + ba_ref[...])
    i = _sigmoid(_dot(xb, wx_ref[...]) + bx_ref[...])
    log_a = (-LRU_C) * r * _softplus(-lam_ref[...])
    a = jnp.exp(log_a)
    u = jnp.sqrt(_one_minus_exp(2.0 * log_a)) * (i * xa)

    rows = lax.broadcasted_iota(jnp.int32, (ts, W), 0)
    span = 1
    while span < ts:
        keep = rows >= span
        a_prev = jnp.where(keep, pltpu.roll(a, span, 0), 1.0)
        u_prev = jnp.where(keep, pltpu.roll(u, span, 0), 0.0)
        u = a * u_prev + u
        a = a * a_prev
        span *= 2
    h = u + a * h_ref[SUBLANES - 1:SUBLANES, :]
    h_ref[...] = h[ts - SUBLANES:, :]
    o_ref[...] = (h * _gelu_tanh(gate)).astype(o_ref.dtype)


def _lru(xg, conv_w, wa, wx, ba, bx, lam, batch):
    T, W2 = xg.shape
    W = W2 // 2
    ts = SCAN_TILE
    nt = T // batch // ts
    return pl.pallas_call(
        _lru_kernel, grid=(batch, nt),
        in_specs=[pl.BlockSpec((ts, W2), lambda b, t: (b * nt + t, 0)),
                  _full(conv_w.shape), _full(wa.shape), _full(wx.shape),
                  _full(ba.shape), _full(bx.shape), _full(lam.shape)],
        out_specs=pl.BlockSpec((ts, W), lambda b, t: (b * nt + t, 0)),
        out_shape=jax.ShapeDtypeStruct((T, W), BF16),
        scratch_shapes=[pltpu.VMEM((SUBLANES + ts, W), F32), pltpu.VMEM((SUBLANES, W), F32)],
        compiler_params=_params("parallel", "arbitrary"), name="rglru",
    )(xg, conv_w, wa, wx, ba, bx, lam)


def _split3(x):
    h1 = x.astype(BF16)
    r1 = x - h1.astype(F32)
    h2 = r1.astype(BF16)
    return h1, h2, (r1 - h2.astype(F32)).astype(BF16)


def _dot_01_right(x, ones):
    h1, h2, h3 = _split3(x)
    return _dot(h1, ones) + _dot(h2, ones) + _dot(h3, ones)


def _dot_01_left(ones, x):
    h1, h2, h3 = _split3(x)
    return _dot(ones, h1) + _dot(ones, h2) + _dot(ones, h3)


def _mlstm_kernel(qk_ref, vo_ref, ifc_ref, ifr_ref, cw_ref, bc_ref, br_ref, expand_ref, lower_ref,
                  upper_ref, o_ref, pad_ref, c_ref, m_ref):
    ts, W = o_ref.shape
    L = MLSTM_CHUNK
    pairs = W // LANES

    @pl.when(pl.program_id(1) == 0)
    def _():
        pad_ref[...] = jnp.zeros_like(pad_ref)
        c_ref[...] = jnp.zeros_like(c_ref)
        m_ref[...] = jnp.zeros_like(m_ref)

    qk = _silu(_causal_conv(pad_ref, qk_ref[...], cw_ref[...]))
    q_all = qk[:, :W].astype(BF16)
    k_all = qk[:, W:] * (HEAD_DIM ** -0.5)
    v_all = vo_ref[:, :W]
    o_gate = _sigmoid(vo_ref[:, W:].astype(F32))

    rep = _dot_01_right(ifc_ref[...] + bc_ref[...], expand_ref[...])
    i_rep = rep[:, :W]
    b_rep = _dot_01_left(lower_ref[...], -_softplus(-rep[:, W:]))
    run = i_rep - b_rep
    pos = jnp.bitwise_and(lax.broadcasted_iota(jnp.int32, (ts, W), 0), L - 1)
    span = 1
    while span < L:
        run = jnp.where(pos >= span, jnp.maximum(run, pltpu.roll(run, span, 0)), run)
        span *= 2
    m_loc = b_rep + run

    lane = lax.broadcasted_iota(jnp.int32, (1, LANES), 1)
    first_head = lane < HEAD_DIM
    r2 = lax.broadcasted_iota(jnp.int32, (LANES, 2 * LANES), 0)
    c2 = lax.broadcasted_iota(jnp.int32, (LANES, 2 * LANES), 1)
    same_head = (r2 >= HEAD_DIM) == (jnp.bitwise_and(c2, LANES - 1) >= HEAD_DIM)
    ones_bd = jnp.where(same_head[:, :LANES], 1.0, 0.0).astype(BF16)
    rl = lax.broadcasted_iota(jnp.int32, (L, LANES), 0)
    cl = lax.broadcasted_iota(jnp.int32, (L, LANES), 1)
    causal2 = jnp.bitwise_and(cl, HEAD_DIM - 1) <= rl

    def stack_heads(a):
        zero = jnp.zeros_like(a)
        return jnp.concatenate([jnp.where(first_head, a, zero), jnp.where(first_head, zero, a)],
                               axis=0)

    m_state = [m_ref[p] for p in range(pairs)]
    c_state = [c_ref[p] for p in range(pairs)]

    for c in range(ts // L):
        rows = slice(c * L, (c + 1) * L)
        pre_r = ifr_ref[0, c] + br_ref[...]
        b_row = _dot_01_right(-_softplus(-pre_r[pairs:]), upper_ref[...])
        for p in range(pairs):
            lanes = slice(p * LANES, (p + 1) * LANES)
            q = q_all[rows, lanes]
            k = k_all[rows, lanes]
            v = v_all[rows, lanes]
            b_c = b_rep[rows, lanes]
            m_l = m_loc[rows, lanes]

            scores = _dot_nt(q, stack_heads(k.astype(BF16)))
            logd = (b_c - m_l) - (b_row[p:p + 1, :] - pre_r[p:p + 1, :])
            pw = (jnp.where(causal2, jnp.exp(logd), 0.0) * scores).astype(BF16)
            local = _dot(pw, jnp.concatenate([stack_heads(v), ones_bd], axis=1))
            b_last = b_c[L - 1:L, :]
            g = b_last - b_c + i_rep[rows, lanes]
            g_max = jnp.max(g, axis=0, keepdims=True)
            kw = (k * jnp.exp(g - g_max)).astype(BF16)
            fresh = _dot_tn(kw, jnp.concatenate([v, jnp.ones_like(v)], axis=1))
            fresh = jnp.where(same_head, fresh, 0.0)

            m_prev = m_state[p]
            carried = _dot(q, c_state[p].astype(BF16))
            m_int = b_c + m_prev
            m_t = jnp.maximum(m_l, m_int)
            a_loc = jnp.exp(m_l - m_t)
            a_int = jnp.exp(m_int - m_t)
            num = a_loc * local[:, :LANES] + a_int * carried[:, :LANES]
            den = a_loc * local[:, LANES:] + a_int * carried[:, LANES:]
            h = num / jnp.maximum(jnp.abs(den), jnp.exp(-m_t))
            o_ref[rows, lanes] = (o_gate[rows, lanes] * h).astype(o_ref.dtype)

            m_new = jnp.maximum(b_last + m_prev, g_max)
            keep = jnp.exp(b_last + m_prev - m_new)
            add = jnp.exp(g_max - m_new)
            c_state[p] = (jnp.concatenate([keep, keep], axis=1) * c_state[p]
                          + jnp.concatenate([add, add], axis=1) * fresh)
            m_state[p] = m_new

    for p in range(pairs):
        c_ref[p] = c_state[p]
        m_ref[p] = m_state[p]


def _mlstm_constants(ts, width, heads):
    L = MLSTM_CHUNK
    expand = np.zeros((LANES, 2 * width), np.float32)
    for g in range(2):
        for h in range(heads):
            expand[g * heads + h, g * width + h * HEAD_DIM:g * width + (h + 1) * HEAD_DIM] = 1.0
    r = np.arange(ts)
    lower = (r[:, None] // L == r[None, :] // L) & (r[None, :] <= r[:, None])
    j = np.arange(LANES)
    upper = (j[:, None] // L == j[None, :] // L) & (j[:, None] <= j[None, :])
    return tuple(jnp.asarray(m, BF16) for m in (expand, lower, upper))


def _mlstm(qk, vo, if_cols, if_rows, conv_w, bias_col, bias_row, batch):
    T, W2 = qk.shape
    W = W2 // 2
    ts = MLSTM_TILE
    nt = T // batch // ts
    nch = ts // MLSTM_CHUNK
    pairs = W // LANES
    consts = _mlstm_constants(ts, W, W // HEAD_DIM)
    tok = lambda b, t: (b * nt + t, 0)
    return pl.pallas_call(
        _mlstm_kernel, grid=(batch, nt),
        in_specs=[pl.BlockSpec((ts, W2), tok), pl.BlockSpec((ts, W2), tok),
                  pl.BlockSpec((ts, LANES), tok),
                  pl.BlockSpec((1, nch, 2 * pairs, LANES), lambda b, t: (b, t, 0, 0)),
                  _full(conv_w.shape), _full(bias_col.shape), _full(bias_row.shape)]
                 + [_full(c.shape) for c in consts],
        out_specs=pl.BlockSpec((ts, W), tok),
        out_shape=jax.ShapeDtypeStruct((T, W), BF16),
        scratch_shapes=[pltpu.VMEM((SUBLANES + ts, W2), F32),
                        pltpu.VMEM((pairs, LANES, 2 * LANES), F32),
                        pltpu.VMEM((pairs, 1, LANES), F32)],
        compiler_params=_params("parallel", "arbitrary"), name="mlstm",
    )(qk, vo, if_cols, if_rows, conv_w, bias_col, bias_row, *consts)


def _sb_kernel(q_ref, k_ref, v_ref, o_ref, z_ref, hl_ref, s_ref, w_ref, acc_ref, live_ref):
    tq = q_ref.shape[0]
    nsub = tq // SB_QSUB
    rows2 = 2 * SB_QSUB
    lane = lax.broadcasted_iota(jnp.int32, (1, LANES), 1)
    first_head = lane < HEAD_DIM

    def suffix_ones(n):
        r = lax.broadcasted_iota(jnp.int32, (n, n), 0)
        c = lax.broadcasted_iota(jnp.int32, (n, n), 1)
        return jnp.where(r >= c, 1.0, 0.0).astype(BF16)

    def window(j):
        t0 = pl.program_id(2) * tq + j * SB_QSUB
        start = pl.multiple_of(jnp.maximum(t0 + SB_QSUB - SB_WINDOW, 0), SB_QSUB)
        return t0, start

    def stacked_queries(j):
        q = q_ref[pl.ds(pl.multiple_of(j * SB_QSUB, SB_QSUB), SB_QSUB), :] * (
            1.0 / math.sqrt(HEAD_DIM))
        zero = jnp.zeros_like(q)
        return jnp.concatenate([jnp.where(first_head, q, zero), jnp.where(first_head, zero, q)],
                               axis=0)

    def split_hi_lo(x):
        hi = x.astype(BF16)
        return hi, (x - hi.astype(F32)).astype(BF16)

    def head_select(stacked):
        return jnp.where(first_head, stacked[:SB_QSUB], stacked[SB_QSUB:])

    ones_window = suffix_ones(SB_WINDOW)
    row = lax.broadcasted_iota(jnp.int32, (rows2, SB_WINDOW), 0)
    col = lax.broadcasted_iota(jnp.int32, (rows2, SB_WINDOW), 1)
    key_minus_query = col - jnp.bitwise_and(row, SB_QSUB - 1)

    for j in range(nsub):
        _, start = window(j)
        z_ref[j] = _dot_nt(stacked_queries(j), k_ref[pl.ds(start, SB_WINDOW), :])
    for j in range(nsub):
        t0, start = window(j)
        log_keep = jnp.where(key_minus_query < t0 - start, -_softplus_fast(z_ref[j]), 0.0)
        hi, lo = split_hi_lo(log_keep)
        hl_ref[j, :rows2, :] = hi
        hl_ref[j, rows2:, :] = lo
    for j in range(nsub):
        s_ref[j] = _dot(hl_ref[j], ones_window)
    any_live = None
    for j in range(nsub):
        t0, start = window(j)
        suffix = s_ref[j, :rows2, :] + s_ref[j, rows2:, :]
        weights = jnp.where(key_minus_query < t0 - start, jnp.exp(z_ref[j] + suffix), 0.0)
        w_ref[j] = weights.astype(BF16)
        live = jnp.logical_and(jnp.max(suffix[:, 0:1]) > SB_LOG_WEIGHT_FLOOR, start > 0)
        live_ref[j] = live.astype(jnp.int32)
        any_live = live if any_live is None else jnp.logical_or(any_live, live)
    for j in range(nsub):
        _, start = window(j)
        acc_ref[j * SB_QSUB:(j + 1) * SB_QSUB, :] = head_select(
            _dot(w_ref[j], v_ref[pl.ds(start, SB_WINDOW), :]))

    @pl.when(any_live)
    def _():
        ones_tail = suffix_ones(SB_TAIL)

        def per_group(j, _):
            @pl.when(live_ref[j] != 0)
            def _():
                _, start = window(j)
                q2 = stacked_queries(j)
                out_rows = pl.ds(pl.multiple_of(j * SB_QSUB, SB_QSUB), SB_QSUB)
                carry0 = s_ref[j, :rows2, 0:1] + s_ref[j, rows2:, 0:1]

                def cond(state):
                    pos, carry, _ = state
                    return jnp.logical_and(pos > 0, jnp.max(carry) > SB_LOG_WEIGHT_FLOOR)

                def body(state):
                    pos, carry, acc = state
                    pos = pos - SB_TAIL
                    keys = pl.ds(pl.multiple_of(pos, SB_TAIL), SB_TAIL)
                    z = _dot_nt(q2, k_ref[keys, :])
                    hi, lo = split_hi_lo(-_softplus_fast(z))
                    suffix = _dot(hi, ones_tail) + _dot(lo, ones_tail)
                    weights = jnp.exp(z + suffix + carry).astype(BF16)
                    return (pos, carry + suffix[:, 0:1],
                            acc + head_select(_dot(weights, v_ref[keys, :])))

                _, _, acc = lax.while_loop(cond, body, (start, carry0, acc_ref[out_rows, :]))
                acc_ref[out_rows, :] = acc
            return 0

        lax.fori_loop(0, nsub, per_group, 0)

    o_ref[...] = acc_ref[...].astype(o_ref.dtype)


def _stick_breaking(qkv, batch):
    T, W3 = qkv.shape
    W = W3 // 3
    S = T // batch
    pairs = W // LANES
    tq = SB_QTILE
    nq = S // tq
    nsub = tq // SB_QSUB
    rows2 = 2 * SB_QSUB
    return pl.pallas_call(
        _sb_kernel, grid=(batch, pairs, nq),
        in_specs=[pl.BlockSpec((tq, LANES), lambda b, p, i: (b * nq + i, p)),
                  pl.BlockSpec((S, LANES), lambda b, p, i: (b, pairs + p)),
                  pl.BlockSpec((S, LANES), lambda b, p, i: (b, 2 * pairs + p))],
        out_specs=pl.BlockSpec((tq, LANES), lambda b, p, i: (b * nq + i, p)),
        out_shape=jax.ShapeDtypeStruct((T, W), BF16),
        scratch_shapes=[pltpu.VMEM((nsub, rows2, SB_WINDOW), F32),
                        pltpu.VMEM((nsub, 2 * rows2, SB_WINDOW), BF16),
                        pltpu.VMEM((nsub, 2 * rows2, SB_WINDOW), F32),
                        pltpu.VMEM((nsub, rows2, SB_WINDOW), BF16),
                        pltpu.VMEM((tq, LANES), F32),
                        pltpu.SMEM((nsub,), jnp.int32)],
        compiler_params=_params("parallel", "parallel", "arbitrary"), name="stickbreak",
    )(qkv, qkv, qkv)


def _merge_ffn_kernel(x_ref, ya_ref, yb_ref, yc_ref, gates_ref, wa_ref, wb_ref, wc_ref, wo_ref,
                      gf_ref, wi_ref, wout_ref, gfin_ref, o_ref, act_ref, *, final_norm):
    D = x_ref.shape[1]
    FF = wout_ref.shape[0]
    merged = jnp.zeros(x_ref.shape, F32)
    for idx, (y_ref, w_ref) in enumerate(((ya_ref, wa_ref), (yb_ref, wb_ref), (yc_ref, wc_ref))):
        gate = _sigmoid(gates_ref[:, idx * D:(idx + 1) * D].astype(F32))
        merged = merged + gate * _dot(y_ref[...], w_ref[...])
    x1 = x_ref[...] + _dot(merged.astype(BF16), wo_ref[...])
    hf = _rmsnorm(x1, gf_ref[...]).astype(BF16)
    c0 = 0
    while c0 < FF:
        c1 = min(c0 + DOT_COLS, FF)
        gate = _dot(hf, wi_ref[:, c0:c1])
        up = _dot(hf, wi_ref[:, FF + c0:FF + c1])
        act_ref[:, c0:c1] = (_silu(gate) * up).astype(BF16)
        c0 = c1
    x2 = x1 + _dot(act_ref[...], wout_ref[...])
    if final_norm:
        x2 = _rmsnorm(x2, gfin_ref[...])
    o_ref[...] = x2


def _merge_ffn(x, ya, yb, yc, gates, wa, wb, wc, wo, gf, wi, wout, gfin, final_norm):
    T, D = x.shape
    tm = TOKEN_TILE
    tok = lambda i: (i, 0)
    return pl.pallas_call(
        functools.partial(_merge_ffn_kernel, final_norm=final_norm), grid=(T // tm,),
        in_specs=[pl.BlockSpec((tm, D), tok), pl.BlockSpec((tm, ya.shape[1]), tok),
                  pl.BlockSpec((tm, yb.shape[1]), tok), pl.BlockSpec((tm, yc.shape[1]), tok),
                  pl.BlockSpec((tm, gates.shape[1]), tok),
                  _full(wa.shape), _full(wb.shape), _full(wc.shape), _full(wo.shape),
                  _full(gf.shape), _full(wi.shape), _full(wout.shape), _full(gfin.shape)],
        out_specs=pl.BlockSpec((tm, D), tok),
        out_shape=jax.ShapeDtypeStruct((T, D), F32),
        scratch_shapes=[pltpu.VMEM((tm, wout.shape[0]), BF16)],
        compiler_params=_params("parallel"), name="merge_ffn",
    )(x, ya, yb, yc, gates, wa, wb, wc, wo, gf, wi, wout, gfin)


def _block_diag(w):
    G, n, _ = w.shape
    eye = jnp.eye(G, dtype=w.dtype)
    return (eye[:, None, :, None] * w[:, :, None, :]).reshape(G * n, G * n)


def kernel(x, norm_mix_g, w_in, conv_lru_w, lru_wa, lru_ba, lru_wx, lru_bx, lru_lambda, conv_mlstm_w, mlstm_ig_b, mlstm_fg_b, w_out_lru, w_out_mlstm, w_out_sb, w_o, norm_ffn_g, w_ffn_in, w_ffn_out, final_norm_g):
    batch, seq, D = x.shape
    depth = w_in.shape[0]
    lru_w = conv_lru_w.shape[-1]
    mlstm_w = conv_mlstm_w.shape[-1] // 2
    heads = mlstm_ig_b.shape[-1]
    sb_w = w_out_sb.shape[1]
    T = batch * seq

    edges = [0]
    for width in (2 * lru_w, 2 * mlstm_w, 2 * mlstm_w, 2 * heads, 3 * sb_w, 3 * D):
        edges.append(edges[-1] + width)
    assert edges[-1] == w_in.shape[-1]

    xt = x.reshape(T, D)
    row = lambda v: v.reshape(1, -1).astype(F32)
    for l in range(depth):
        w = w_in[l]
        w_lru, w_mqk, w_mvo, w_if, w_sqkv, w_gates = (
            w[:, edges[i]:edges[i + 1]].astype(BF16) for i in range(6))
        w_if = jnp.pad(w_if, ((0, 0), (0, LANES - 2 * heads)))
        lru_xg, m_qk, m_vo, m_if, s_qkv, gates = _inproj(
            xt, row(norm_mix_g[l]), (w_lru, w_mqk, w_mvo, w_if, w_sqkv, w_gates),
            (F32, F32, BF16, F32, BF16, BF16))

        ya = _lru(lru_xg, conv_lru_w[l].astype(F32), _block_diag(lru_wa[l]).astype(BF16),
                  _block_diag(lru_wx[l]).astype(BF16), row(lru_ba[l]), row(lru_bx[l]),
                  row(lru_lambda[l]), batch)

        gate_bias = jnp.concatenate([mlstm_ig_b[l], mlstm_fg_b[l]]).astype(F32)
        bias_col = jnp.pad(gate_bias, (0, LANES - 2 * heads)).reshape(1, LANES)
        bias_row = jnp.repeat(gate_bias, MLSTM_CHUNK).reshape(heads, 2 * MLSTM_CHUNK)
        if_rows = (m_if[:, :2 * heads]
                   .reshape(batch, seq // MLSTM_CHUNK, MLSTM_CHUNK, heads, 2)
                   .transpose(0, 1, 3, 4, 2)
                   .reshape(batch, seq // MLSTM_CHUNK, heads, 2 * MLSTM_CHUNK))
        yb = _mlstm(m_qk, m_vo, m_if, if_rows, conv_mlstm_w[l].astype(F32), bias_col, bias_row,
                    batch)

        yc = _stick_breaking(s_qkv, batch)

        xt = _merge_ffn(xt, ya, yb, yc, gates, w_out_lru[l].astype(BF16),
                        w_out_mlstm[l].astype(BF16), w_out_sb[l].astype(BF16),
                        w_o[l].astype(BF16), row(norm_ffn_g[l]), w_ffn_in[l].astype(BF16),
                        w_ffn_out[l].astype(BF16), row(final_norm_g), final_norm=(l == depth - 1))
    return xt.reshape(batch, seq, D)
```

```python
import functools
import math

import jax
import jax.numpy as jnp
import numpy as np
from jax import lax
from jax.experimental import pallas as pl
from jax.experimental.pallas import tpu as pltpu

F32 = jnp.float32
BF16 = jnp.bfloat16

LANES = 128
SUBLANES = 8
VMEM_LIMIT_BYTES = 56 * 1024 * 1024

EPS = 1e-6
LRU_C = 8.0
CONV_W = 4
HEAD_DIM = 64
MLSTM_CHUNK = 64
MLSTM_CUMSUM_ROWS = 256
SB_QSUB = 64
SB_WINDOW = 256
SB_TAIL = 64
SB_LOG_WEIGHT_FLOOR = -110.0

TOKEN_TILE = 512
SB_QTILE = 512
DOT_COLS = 512


def _params(*semantics):
    return pltpu.CompilerParams(dimension_semantics=semantics, vmem_limit_bytes=VMEM_LIMIT_BYTES)


def _full(shape):
    return pl.BlockSpec(shape, lambda *_: (0,) * len(shape))


def _rmsnorm(x, g):
    return x * lax.rsqrt(jnp.mean(x * x, axis=-1, keepdims=True) + EPS) * g


def _softplus(x):
    return jnp.maximum(x, 0.0) + jnp.log1p(jnp.exp(-jnp.abs(x)))


def _softplus_fast(x):
    return jnp.maximum(x, 0.0) + jnp.log(1.0 + jnp.exp(-jnp.abs(x)))


def _sigmoid(x):
    return 1.0 / (1.0 + jnp.exp(-x))


def _silu(x):
    return x * _sigmoid(x)


def _gelu_tanh(x):
    return 0.5 * x * (1.0 + jnp.tanh(math.sqrt(2.0 / math.pi) * (x + 0.044715 * (x * x * x))))


def _one_minus_exp(y):
    poly = jnp.full_like(y, 1.0 / math.factorial(7))
    for n in range(6, 0, -1):
        poly = poly * y + 1.0 / math.factorial(n)
    return jnp.where(y > -0.125, -(y * poly), 1.0 - jnp.exp(y))


def _dot(a, b):
    return jnp.dot(a, b, preferred_element_type=F32)


def _dot_nt(a, b):
    return lax.dot_general(a, b, (((1,), (1,)), ((), ())), preferred_element_type=F32)


def _dot_tn(a, b):
    return lax.dot_general(a, b, (((0,), (0,)), ((), ())), preferred_element_type=F32)


def _split3(x):
    h1 = x.astype(BF16)
    r1 = x - h1.astype(F32)
    h2 = r1.astype(BF16)
    return h1, h2, (r1 - h2.astype(F32)).astype(BF16)


def _dot_01_left(ones, x):
    h1, h2, h3 = _split3(x)
    return _dot(ones, h1) + _dot(ones, h2) + _dot(ones, h3)


def _causal_conv(pad_ref, x, w):
    ts = x.shape[0]
    pad_ref[0:SUBLANES, :] = pad_ref[ts:ts + SUBLANES, :]
    pad_ref[SUBLANES:, :] = x
    out = x * w[CONV_W - 1:CONV_W, :]
    for back in range(1, CONV_W):
        out = out + (pad_ref[SUBLANES - back:SUBLANES - back + ts, :]
                     * w[CONV_W - 1 - back:CONV_W - back, :])
    return out


def _lru_tile(xg_ref, cw_ref, wa_ref, wx_ref, ba_ref, bx_ref, lam_ref, o_ref, pad_ref, h_ref):
    ts, W = o_ref.shape
    gate = xg_ref[:, W:]
    xa = _causal_conv(pad_ref, xg_ref[:, :W], cw_ref[...])

    xb = xa.astype(BF16)
    r = _sigmoid(_dot(xb, wa_ref[...]) + ba_ref[...])
    i = _sigmoid(_dot(xb, wx_ref[...]) + bx_ref[...])
    log_a = (-LRU_C) * r * _softplus(-lam_ref[...])
    a = jnp.exp(log_a)
    u = jnp.sqrt(_one_minus_exp(2.0 * log_a)) * (i * xa)

    rows = lax.broadcasted_iota(jnp.int32, (ts, W), 0)
    span = 1
    while span < ts:
        keep = rows >= span
        a_prev = jnp.where(keep, pltpu.roll(a, span, 0), 1.0)
        u_prev = jnp.where(keep, pltpu.roll(u, span, 0), 0.0)
        u = a * u_prev + u
        a = a * a_prev
        span *= 2
    h = u + a * h_ref[SUBLANES - 1:SUBLANES, :]
    h_ref[...] = h[ts - SUBLANES:, :]
    o_ref[...] = (h * _gelu_tanh(gate)).astype(o_ref.dtype)


def _mlstm_tile(qk_ref, vo_ref, if_ref, cw_ref, bias_ref, lower_ref, o_ref, pad_ref, c_ref, m_ref):
    ts, W = o_ref.shape
    L = MLSTM_CHUNK
    pairs = W // LANES
    heads = W // HEAD_DIM

    qk =_silu(_causal_conv(pad_ref, qk_ref[...], cw_ref[...]))
    q_all = qk[:, :W].astype(BF16)
    k_all = qk[:, W:] * (HEAD_DIM ** -0.5)
    v_all = vo_ref[:, :W]
    o_gate = _sigmoid(vo_ref[:, W:].astype(F32))

    lane = lax.broadcasted_iota(jnp.int32, (1, LANES), 1)
    first_head = lane < HEAD_DIM
    r2 = lax.broadcasted_iota(jnp.int32, (LANES, 2 * LANES), 0)
    c2 = lax.broadcasted_iota(jnp.int32, (LANES, 2 * LANES), 1)
    same_head = (r2 >= HEAD_DIM) == (jnp.bitwise_and(c2, LANES - 1) >= HEAD_DIM)
    ones_bd = jnp.where(same_head[:, :LANES], 1.0, 0.0).astype(BF16)
    rl = lax.broadcasted_iota(jnp.int32, (L, LANES), 0)
    key = jnp.bitwise_and(lax.broadcasted_iota(jnp.int32, (L, LANES), 1), HEAD_DIM - 1)
    causal2 = key <= rl
    diagonal2 = key == rl

    def stack_heads(a):
        zero = jnp.zeros_like(a)
        return jnp.concatenate([jnp.where(first_head, a, zero), jnp.where(first_head, zero, a)],
                               axis=0)

    pre = if_ref[...] + bias_ref[...]
    logf = -_softplus(-pre)
    pos = jnp.bitwise_and(lax.broadcasted_iota(jnp.int32, (ts, LANES), 0), L - 1)
    i_rep, b_rep, m_loc = [], [], []
    for p in range(pairs):
        i_p = jnp.where(first_head, pre[:, 2 * p:2 * p + 1], pre[:, 2 * p + 1:2 * p + 2])
        f_p = jnp.where(first_head, logf[:, heads + 2 * p:heads + 2 * p + 1],
                        logf[:, heads + 2 * p + 1:heads + 2 * p + 2])
        b_p = jnp.concatenate(
            [_dot_01_left(lower_ref[...], f_p[r0:r0 + MLSTM_CUMSUM_ROWS])
             for r0 in range(0, ts, MLSTM_CUMSUM_ROWS)], axis=0)
        run = i_p - b_p
        span = 1
        while span < L:
            run = jnp.where(pos >= span, jnp.maximum(run, pltpu.roll(run, span, 0)), run)
            span *= 2
        i_rep.append(i_p)
        b_rep.append(b_p)
        m_loc.append(b_p + run)

    m_state = [m_ref[p] for p in range(pairs)]
    c_state = [c_ref[p] for p in range(pairs)]

    for c in range(ts // L):
        rows = slice(c * L, (c + 1) * L)
        for p in range(pairs):
            lanes = slice(p * LANES, (p + 1) * LANES)
            q = q_all[rows, lanes]
            k = k_all[rows, lanes]
            v = v_all[rows, lanes]
            b_c = b_rep[p][rows]
            i_c = i_rep[p][rows]
            m_l = m_loc[p][rows]

            scores = _dot_nt(q, stack_heads(k.astype(BF16)))
            key_term = jnp.sum(jnp.where(diagonal2, b_c - i_c, 0.0), axis=0, keepdims=True)
            logd = (b_c - m_l) - key_term
            pw = (jnp.where(causal2, jnp.exp(logd), 0.0) * scores).astype(BF16)
            local = _dot(pw, jnp.concatenate([stack_heads(v), ones_bd], axis=1))
            b_last = b_c[L - 1:L, :]
            g = b_last - b_c + i_c
            g_max = jnp.max(g, axis=0, keepdims=True)
            kw = (k * jnp.exp(g - g_max)).astype(BF16)
            fresh = _dot_tn(kw, jnp.concatenate([v, jnp.ones_like(v)], axis=1))
            fresh = jnp.where(same_head, fresh, 0.0)

            m_prev = m_state[p]
            carried = _dot(q, c_state[p].astype(BF16))
            m_int = b_c + m_prev
            m_t = jnp.maximum(m_l, m_int)
            a_loc = jnp.exp(m_l - m_t)
            a_int = jnp.exp(m_int - m_t)
            num = a_loc * local[:, :LANES] + a_int * carried[:, :LANES]
            den = a_loc * local[:, LANES:] + a_int * carried[:, LANES:]
            h = num / jnp.maximum(jnp.abs(den), jnp.exp(-m_t))
            o_ref[rows, lanes] = (o_gate[rows, lanes] * h).astype(o_ref.dtype)

            m_new = jnp.maximum(b_last + m_prev, g_max)
            keep = jnp.exp(b_last + m_prev - m_new)
            add = jnp.exp(g_max - m_new)
            c_state[p] = (jnp.concatenate([keep, keep], axis=1) * c_state[p]
                          + jnp.concatenate([add, add], axis=1) * fresh)
            m_state[p] = m_new

    for p in range(pairs):
        c_ref[p] = c_state[p]
        m_ref[p] = m_state[p]


def _project(h, w_ref, first_col, o_ref):
    cols = o_ref.shape[1]
    for c0 in range(0, cols, DOT_COLS):
        c1 = min(c0 + DOT_COLS, cols)
        o_ref[:, c0:c1] = _dot(h, w_ref[:, first_col + c0:first_col + c1]).astype(o_ref.dtype)


def _mixer_in_kernel(x_ref, g_ref, w_ref, lcw_ref, wa_ref, wx_ref, ba_ref, bx_ref, lam_ref, mcw_ref,
                     mbias_ref, lower_ref, ya_ref, yb_ref, sqkv_ref, gates_ref,
                     xg_s, qk_s, vo_s, if_s, lru_pad, lru_h, m_pad, m_c, m_m, *, tiles_per_seq):
    @pl.when(lax.rem(pl.program_id(0), tiles_per_seq) == 0)
    def _():
        for ref in (lru_pad, lru_h, m_pad, m_c, m_m):
            ref[...] = jnp.zeros_like(ref)

    h = _rmsnorm(x_ref[...], g_ref[...]).astype(BF16)
    first_col = 0
    for o_ref in (xg_s, qk_s, vo_s, if_s, sqkv_ref, gates_ref):
        _project(h, w_ref, first_col, o_ref)
        first_col += o_ref.shape[1]
    _lru_tile(xg_s, lcw_ref, wa_ref, wx_ref, ba_ref, bx_ref, lam_ref, ya_ref, lru_pad, lru_h)
    _mlstm_tile(qk_s, vo_s, if_s, mcw_ref, mbias_ref, lower_ref, yb_ref, m_pad, m_c, m_m)


def _cumsum_matrix():
    r = np.arange(MLSTM_CUMSUM_ROWS)
    lower = (r[:, None] // MLSTM_CHUNK == r[None, :] // MLSTM_CHUNK) & (r[None, :] <= r[:, None])
    return jnp.asarray(lower, BF16)


def _mixer_in(x, g, w, lru_params, mlstm_params, sb_w, batch):
    T, D = x.shape
    tm = TOKEN_TILE
    lru_w = lru_params[0].shape[1]
    mlstm_w = mlstm_params[0].shape[1] // 2
    pairs = mlstm_w // LANES
    params = (w, *lru_params, *mlstm_params, _cumsum_matrix())
    tok = lambda i: (i, 0)
    out_widths = (lru_w, mlstm_w, 3 * sb_w, 3 * D)
    assert w.shape[1] == 2 * lru_w + 4 * mlstm_w + LANES + 3 * sb_w + 3 * D
    return pl.pallas_call(
        functools.partial(_mixer_in_kernel, tiles_per_seq=T // batch // tm), grid=(T // tm,),
        in_specs=[pl.BlockSpec((tm, D), tok), _full(g.shape)] + [_full(a.shape) for a in params],
        out_specs=[pl.BlockSpec((tm, n), tok) for n in out_widths],
        out_shape=[jax.ShapeDtypeStruct((T, n), BF16) for n in out_widths],
        scratch_shapes=[pltpu.VMEM((tm, 2 * lru_w), F32),
                        pltpu.VMEM((tm, 2 * mlstm_w), F32),
                        pltpu.VMEM((tm, 2 * mlstm_w), BF16),
                        pltpu.VMEM((tm, LANES), F32),
                        pltpu.VMEM((SUBLANES + tm, lru_w), F32),
                        pltpu.VMEM((SUBLANES, lru_w), F32),
                        pltpu.VMEM((SUBLANES + tm, 2 * mlstm_w), F32),
                        pltpu.VMEM((pairs, LANES, 2 * LANES), F32),
                        pltpu.VMEM((pairs, 1, LANES), F32)],
        compiler_params=_params("arbitrary"), name="mixer_in",
    )(x, g, *params)


def _sb_kernel(q_ref, k_ref, v_ref, o_ref, z_ref, hl_ref, s_ref, w_ref, acc_ref, live_ref):
    tq = q_ref.shape[0]
    nsub = tq // SB_QSUB
    rows2 = 2 * SB_QSUB
    lane = lax.broadcasted_iota(jnp.int32, (1, LANES), 1)
    first_head = lane < HEAD_DIM

    def suffix_ones(n):
        r = lax.broadcasted_iota(jnp.int32, (n, n), 0)
        c = lax.broadcasted_iota(jnp.int32, (n, n), 1)
        return jnp.where(r >= c, 1.0, 0.0).astype(BF16)

    def window(j):
        t0 = pl.program_id(2) * tq + j * SB_QSUB
        start = pl.multiple_of(jnp.maximum(t0 + SB_QSUB - SB_WINDOW, 0), SB_QSUB)
        return t0, start

    def stacked_queries(j):
        q = q_ref[pl.ds(pl.multiple_of(j * SB_QSUB, SB_QSUB), SB_QSUB), :] * (
            1.0 / math.sqrt(HEAD_DIM))
        zero = jnp.zeros_like(q)
        return jnp.concatenate([jnp.where(first_head, q, zero), jnp.where(first_head, zero, q)],
                               axis=0)

    def split_hi_lo(x):
        hi = x.astype(BF16)
        return hi, (x - hi.astype(F32)).astype(BF16)

    def head_select(stacked):
        return jnp.where(first_head, stacked[:SB_QSUB], stacked[SB_QSUB:])

    ones_window = suffix_ones(SB_WINDOW)
    row = lax.broadcasted_iota(jnp.int32, (rows2, SB_WINDOW), 0)
    col = lax.broadcasted_iota(jnp.int32, (rows2, SB_WINDOW), 1)
    key_minus_query = col - jnp.bitwise_and(row, SB_QSUB - 1)

    for j in range(nsub):
        _, start = window(j)
        z_ref[j] = _dot_nt(stacked_queries(j), k_ref[pl.ds(start, SB_WINDOW), :])
    for j in range(nsub):
        t0, start = window(j)
        log_keep = jnp.where(key_minus_query < t0 - start, -_softplus_fast(z_ref[j]), 0.0)
        hi, lo = split_hi_lo(log_keep)
        hl_ref[j, :rows2, :] = hi
        hl_ref[j, rows2:, :] = lo
    for j in range(nsub):
        s_ref[j] = _dot(hl_ref[j], ones_window)
    any_live = None
    for j in range(nsub):
        t0, start = window(j)
        suffix = s_ref[j, :rows2, :] + s_ref[j, rows2:, :]
        weights = jnp.where(key_minus_query < t0 - start, jnp.exp(z_ref[j] + suffix), 0.0)
        w_ref[j] = weights.astype(BF16)
        live = jnp.logical_and(jnp.max(suffix[:, 0:1]) > SB_LOG_WEIGHT_FLOOR, start > 0)
        live_ref[j] = live.astype(jnp.int32)
        any_live = live if any_live is None else jnp.logical_or(any_live, live)
    for j in range(nsub):
        _, start = window(j)
        acc_ref[j * SB_QSUB:(j + 1) * SB_QSUB, :] = head_select(
            _dot(w_ref[j], v_ref[pl.ds(start, SB_WINDOW), :]))

    @pl.when(any_live)
    def _():
        ones_tail = suffix_ones(SB_TAIL)

        def per_group(j, _):
            @pl.when(live_ref[j] != 0)
            def _():
                _, start = window(j)
                q2 = stacked_queries(j)
                out_rows = pl.ds(pl.multiple_of(j * SB_QSUB, SB_QSUB), SB_QSUB)
                carry0 = s_ref[j, :rows2, 0:1] + s_ref[j, rows2:, 0:1]

                def cond(state):
                    pos, carry, _ = state
                    return jnp.logical_and(pos > 0, jnp.max(carry) > SB_LOG_WEIGHT_FLOOR)

                def body(state):
                    pos, carry, acc = state
                    pos = pos - SB_TAIL
                    keys = pl.ds(pl.multiple_of(pos, SB_TAIL), SB_TAIL)
                    z = _dot_nt(q2, k_ref[keys, :])
                    hi, lo = split_hi_lo(-_softplus_fast(z))
                    suffix = _dot(hi, ones_tail) + _dot(lo, ones_tail)
                    weights = jnp.exp(z + suffix + carry).astype(BF16)
                    return (pos, carry + suffix[:, 0:1],
                            acc + head_select(_dot(weights, v_ref[keys, :])))

                _, _, acc = lax.while_loop(cond, body, (start, carry0, acc_ref[out_rows, :]))
                acc_ref[out_rows, :] = acc
            return 0

        lax.fori_loop(0, nsub, per_group, 0)

    o_ref[...] = acc_ref[...].astype(o_ref.dtype)


def _stick_breaking(qkv, batch):
    T, W3 = qkv.shape
    W = W3 // 3
    S = T // batch
    pairs = W // LANES
    tq = SB_QTILE
    nq = S // tq
    nsub = tq // SB_QSUB
    rows2 = 2 * SB_QSUB
    return pl.pallas_call(
        _sb_kernel, grid=(batch, pairs, nq),
        in_specs=[pl.BlockSpec((tq, LANES), lambda b, p, i: (b * nq + i, p)),
                  pl.BlockSpec((S, LANES), lambda b, p, i: (b, pairs + p)),
                  pl.BlockSpec((S, LANES), lambda b, p, i: (b, 2 * pairs + p))],
        out_specs=pl.BlockSpec((tq, LANES), lambda b, p, i: (b * nq + i, p)),
        out_shape=jax.ShapeDtypeStruct((T, W), BF16),
        scratch_shapes=[pltpu.VMEM((nsub, rows2, SB_WINDOW), F32),
                        pltpu.VMEM((nsub, 2 * rows2, SB_WINDOW), BF16),
                        pltpu.VMEM((nsub, 2 * rows2, SB_WINDOW), F32),
                        pltpu.VMEM((nsub, rows2, SB_WINDOW), BF16),
                        pltpu.VMEM((tq, LANES), F32),
                        pltpu.SMEM((nsub,), jnp.int32)],
        compiler_params=_params("parallel", "parallel", "arbitrary"), name="stickbreak",
    )(qkv, qkv, qkv)


def _merge_ffn_kernel(x_ref, ya_ref, yb_ref, yc_ref, gates_ref, wa_ref, wb_ref, wc_ref, wo_ref,
                      gf_ref, wi_ref, wout_ref, gfin_ref, o_ref, act_ref, *, final_norm):
    D = x_ref.shape[1]
    FF = wout_ref.shape[0]
    merged = jnp.zeros(x_ref.shape, F32)
    for idx, (y_ref, w_ref) in enumerate(((ya_ref, wa_ref), (yb_ref, wb_ref), (yc_ref, wc_ref))):
        gate = _sigmoid(gates_ref[:, idx * D:(idx + 1) * D].astype(F32))
        merged = merged + gate * _dot(y_ref[...], w_ref[...])
    x1 = x_ref[...] + _dot(merged.astype(BF16), wo_ref[...])
    hf = _rmsnorm(x1, gf_ref[...]).astype(BF16)
    c0 = 0
    while c0 < FF:
        c1 = min(c0 + DOT_COLS, FF)
        gate = _dot(hf, wi_ref[:, c0:c1])
        up = _dot(hf, wi_ref[:, FF + c0:FF + c1])
        act_ref[:, c0:c1] = (_silu(gate) * up).astype(BF16)
        c0 = c1
    x2 = x1 + _dot(act_ref[...], wout_ref[...])
    if final_norm:
        x2 = _rmsnorm(x2, gfin_ref[...])
    o_ref[...] = x2


def _merge_ffn(x, ya, yb, yc, gates, wa, wb, wc, wo, gf, wi, wout, gfin, final_norm):
    T, D = x.shape
    tm = TOKEN_TILE
    tok = lambda i: (i, 0)
    return pl.pallas_call(
        functools.partial(_merge_ffn_kernel, final_norm=final_norm), grid=(T // tm,),
        in_specs=[pl.BlockSpec((tm, D), tok), pl.BlockSpec((tm, ya.shape[1]), tok),
                  pl.BlockSpec((tm, yb.shape[1]), tok), pl.BlockSpec((tm, yc.shape[1]), tok),
                  pl.BlockSpec((tm, gates.shape[1]), tok),
                  _full(wa.shape), _full(wb.shape), _full(wc.shape), _full(wo.shape),
                  _full(gf.shape), _full(wi.shape), _full(wout.shape), _full(gfin.shape)],
        out_specs=pl.BlockSpec((tm, D), tok),
        out_shape=jax.ShapeDtypeStruct((T, D), F32),
        scratch_shapes=[pltpu.VMEM((tm, wout.shape[0]), BF16)],
        compiler_params=_params("parallel"), name="merge_ffn",
    )(x, ya, yb, yc, gates, wa, wb, wc, wo, gf, wi, wout, gfin)


def _block_diag(w):
    G, n, _ = w.shape
    eye = jnp.eye(G, dtype=w.dtype)
    return (eye[:, None, :, None] * w[:, :, None, :]).reshape(G * n, G * n)


def kernel(x, norm_mix_g, w_in, conv_lru_w, lru_wa, lru_ba, lru_wx, lru_bx, lru_lambda, conv_mlstm_w, mlstm_ig_b, mlstm_fg_b, w_out_lru, w_out_mlstm, w_out_sb, w_o, norm_ffn_g, w_ffn_in, w_ffn_out, final_norm_g):
    batch, seq, D = x.shape
    depth = w_in.shape[0]
    lru_w = conv_lru_w.shape[-1]
    mlstm_w = conv_mlstm_w.shape[-1] // 2
    heads = mlstm_ig_b.shape[-1]
    sb_w = w_out_sb.shape[1]
    T = batch * seq

    gate_lo = 2 * lru_w + 4 * mlstm_w
    gate_hi = gate_lo + 2 * heads
    assert gate_hi + 3 * sb_w + 3 * D == w_in.shape[-1]
    w_in_b = jnp.concatenate(
        [w_in[..., :gate_hi].astype(BF16),
         jnp.zeros((depth, D, LANES - 2 * heads), BF16),
         w_in[..., gate_hi:].astype(BF16)], axis=-1)
    w_a, w_b, w_c, w_o_b, w_fi, w_fo = (
        w.astype(BF16) for w in (w_out_lru, w_out_mlstm, w_out_sb, w_o, w_ffn_in, w_ffn_out))
    gate_bias = jnp.pad(jnp.concatenate([mlstm_ig_b, mlstm_fg_b], axis=-1).astype(F32),
                        ((0, 0), (0, LANES - 2 * heads)))

    xt = x.reshape(T, D)
    row = lambda v: v.reshape(1, -1).astype(F32)
    for l in range(depth):
        ya, yb, s_qkv, gates = _mixer_in(
            xt, row(norm_mix_g[l]), w_in_b[l],
            (conv_lru_w[l].astype(F32), _block_diag(lru_wa[l]).astype(BF16),
             _block_diag(lru_wx[l]).astype(BF16), row(lru_ba[l]), row(lru_bx[l]),
             row(lru_lambda[l])),
            (conv_mlstm_w[l].astype(F32), row(gate_bias[l])), sb_w, batch)

        yc = _stick_breaking(s_qkv, batch)

        xt = _merge_ffn(xt, ya, yb, yc, gates, w_a[l], w_b[l], w_c[l], w_o_b[l],
                        row(norm_ffn_g[l]), w_fi[l], w_fo[l], row(final_norm_g),
                        final_norm=(l == depth - 1))
    return xt.reshape(batch, seq, D)
```

```python
import functools
import math

import jax
import jax.numpy as jnp
import numpy as np
from jax import lax
from jax.experimental import pallas as pl
from jax.experimental.pallas import tpu as pltpu

F32 = jnp.float32
BF16 = jnp.bfloat16

LANES = 128
SUBLANES = 8
VMEM_LIMIT_BYTES = 56 * 1024 * 1024

EPS = 1e-6
LRU_C = 8.0
CONV_W = 4
HEAD_DIM = 64
LRU_BLOCK = 16
MLSTM_CHUNK = 64
MLSTM_CUMSUM_ROWS = 256
SB_QSUB = 64
SB_WINDOW = 256
SB_TAIL = 64
SB_LOG_WEIGHT_FLOOR = -110.0

TOKEN_TILE = 512
SB_QTILE = 512
DOT_COLS = 512
PIECE_COLS = 256


def _params(*semantics):
    return pltpu.CompilerParams(dimension_semantics=semantics, vmem_limit_bytes=VMEM_LIMIT_BYTES)


def _full(shape):
    return pl.BlockSpec(shape, lambda *_: (0,) * len(shape))


def _layer(stacked, l):
    zeros = (0,) * (stacked.ndim - 1)
    return pl.BlockSpec((None,) + stacked.shape[1:], lambda *_: (l,) + zeros,
                        pipeline_mode=pl.Buffered(1))


def _rmsnorm(x, g):
    return x * lax.rsqrt(jnp.mean(x * x, axis=-1, keepdims=True) + EPS) * g


def _softplus(x):
    return jnp.maximum(x, 0.0) + jnp.log1p(jnp.exp(-jnp.abs(x)))


def _softplus_fast(x):
    return jnp.maximum(x, 0.0) + jnp.log(1.0 + jnp.exp(-jnp.abs(x)))


def _sigmoid(x):
    return 1.0 / (1.0 + jnp.exp(-x))


def _silu(x):
    return x * _sigmoid(x)


def _gelu_tanh(x):
    return 0.5 * x * (1.0 + jnp.tanh(math.sqrt(2.0 / math.pi) * (x + 0.044715 * (x * x * x))))


def _one_minus_exp(y):
    poly = jnp.full_like(y, 1.0 / math.factorial(7))
    for n in range(6, 0, -1):
        poly = poly * y + 1.0 / math.factorial(n)
    return jnp.where(y > -0.125, -(y * poly), 1.0 - jnp.exp(y))


def _dot(a, b):
    return jnp.dot(a, b, preferred_element_type=F32)


def _dot_nt(a, b):
    return lax.dot_general(a, b, (((1,), (1,)), ((), ())), preferred_element_type=F32)


def _dot_tn(a, b):
    return lax.dot_general(a, b, (((0,), (0,)), ((), ())), preferred_element_type=F32)


def _split3(x):
    h1 = x.astype(BF16)
    r1 = x - h1.astype(F32)
    h2 = r1.astype(BF16)
    return h1, h2, (r1 - h2.astype(F32)).astype(BF16)


def _dot_01_left(ones, x):
    h1, h2, h3 = _split3(x)
    return _dot(ones, h1) + _dot(ones, h2) + _dot(ones, h3)


def _causal_conv(pad_ref, x, w):
    ts = x.shape[0]
    pad_ref[0:SUBLANES, :] = pad_ref[ts:ts + SUBLANES, :]
    pad_ref[SUBLANES:, :] = x
    out = x * w[CONV_W - 1:CONV_W, :]
    for back in range(1, CONV_W):
        out = out + (pad_ref[SUBLANES - back:SUBLANES - back + ts, :]
                     * w[CONV_W - 1 - back:CONV_W - back, :])
    return out


def _lru_tile(xg_ref, cw_ref, wa_ref, wx_ref, ba_ref, bx_ref, lam_ref, o_ref, pad_ref, h_ref,
              xa_ref, r_ref, i_ref, fill):
    ts, W = o_ref.shape
    xa = _causal_conv(pad_ref, xg_ref[:, :W], cw_ref[...])
    xa_ref[...] = xa
    fill(2)
    xb = xa.astype(BF16)
    r_ref[...] = _dot(xb, wa_ref[...]) + ba_ref[...]
    i_ref[...] = _dot(xb, wx_ref[...]) + bx_ref[...]
    token = fill(2)

    log_decay = (-LRU_C) * _softplus(-lam_ref[...])
    row = lax.broadcasted_iota(jnp.int32, (LRU_BLOCK, W), 0)
    carry = h_ref[SUBLANES - 1:SUBLANES, :] + jnp.concatenate([token] * (W // LANES), axis=1)
    blocks = ts // LRU_BLOCK
    for blk in range(blocks):
        rows = slice(blk * LRU_BLOCK, (blk + 1) * LRU_BLOCK)
        log_a = log_decay * _sigmoid(r_ref[rows, :])
        a = jnp.exp(log_a)
        u = jnp.sqrt(_one_minus_exp(2.0 * log_a)) * (_sigmoid(i_ref[rows, :]) * xa_ref[rows, :])
        span = 1
        while span < LRU_BLOCK:
            keep = row >= span
            a_prev = jnp.where(keep, pltpu.roll(a, span, 0), 1.0)
            u_prev = jnp.where(keep, pltpu.roll(u, span, 0), 0.0)
            u = a * u_prev + u
            a = a * a_prev
            span *= 2
        h = u + a * carry
        carry = h[LRU_BLOCK - 1:LRU_BLOCK, :]
        o_ref[rows, :] = (h * _gelu_tanh(xg_ref[rows, W:])).astype(o_ref.dtype)
        if blk % (blocks // 4) == blocks // 4 - 1:
            token = fill(1)
            carry = carry + jnp.concatenate([token] * (W // LANES), axis=1)
    h_ref[...] = jnp.broadcast_to(carry, h_ref.shape)


def _mlstm_tile(qk_ref, vo_ref, if_ref, cw_ref, bias_ref, lower_ref, o_ref, pad_ref, c_ref, m_ref,
                fill):
    ts, W = o_ref.shape
    L = MLSTM_CHUNK
    pairs = W // LANES
    heads = W // HEAD_DIM

    qk = _silu(_causal_conv(pad_ref, qk_ref[...], cw_ref[...]))
    fill(1)
    q_all = qk[:, :W].astype(BF16)
    k_all = qk[:, W:] * (HEAD_DIM ** -0.5)
    v_all = vo_ref[:, :W]
    o_gate = _sigmoid(vo_ref[:, W:].astype(F32))

    lane = lax.broadcasted_iota(jnp.int32, (1, LANES), 1)
    first_head = lane < HEAD_DIM
    r2 = lax.broadcasted_iota(jnp.int32, (LANES, 2 * LANES), 0)
    c2 = lax.broadcasted_iota(jnp.int32, (LANES, 2 * LANES), 1)
    same_head = (r2 >= HEAD_DIM) == (jnp.bitwise_and(c2, LANES - 1) >= HEAD_DIM)
    ones_bd = jnp.where(same_head[:, :LANES], 1.0, 0.0).astype(BF16)
    rl = lax.broadcasted_iota(jnp.int32, (L, LANES), 0)
    key = jnp.bitwise_and(lax.broadcasted_iota(jnp.int32, (L, LANES), 1), HEAD_DIM - 1)
    causal2 = key <= rl
    diagonal2 = key == rl

    def stack_heads(a):
        zero = jnp.zeros_like(a)
        return jnp.concatenate([jnp.where(first_head, a, zero), jnp.where(first_head, zero, a)],
                               axis=0)

    pre = if_ref[...] + bias_ref[...]
    logf = -_softplus(-pre)
    pos = jnp.bitwise_and(lax.broadcasted_iota(jnp.int32, (ts, LANES), 0), L - 1)
    i_rep, b_rep, m_loc = [], [], []
    for p in range(pairs):
        i_p = jnp.where(first_head, pre[:, 2 * p:2 * p + 1], pre[:, 2 * p + 1:2 * p + 2])
        f_p = jnp.where(first_head, logf[:, heads + 2 * p:heads + 2 * p + 1],
                        logf[:, heads + 2 * p + 1:heads + 2 * p + 2])
        b_p = jnp.concatenate(
            [_dot_01_left(lower_ref[...], f_p[r0:r0 + MLSTM_CUMSUM_ROWS])
             for r0 in range(0, ts, MLSTM_CUMSUM_ROWS)], axis=0)
        run = i_p - b_p
        span = 1
        while span < L:
            run = jnp.where(pos >= span, jnp.maximum(run, pltpu.roll(run, span, 0)), run)
            span *= 2
        fill(1)
        i_rep.append(i_p)
        b_rep.append(b_p)
        m_loc.append(b_p + run)

    m_state = [m_ref[p] for p in range(pairs)]
    c_state = [c_ref[p] for p in range(pairs)]

    for c in range(ts // L):
        rows = slice(c * L, (c + 1) * L)
        for p in range(pairs):
            lanes = slice(p * LANES, (p + 1) * LANES)
            q = q_all[rows, lanes]
            k = k_all[rows, lanes]
            v = v_all[rows, lanes]
            b_c = b_rep[p][rows]
            i_c = i_rep[p][rows]
            m_l = m_loc[p][rows]

            scores = _dot_nt(q, stack_heads(k.astype(BF16)))
            key_term = jnp.sum(jnp.where(diagonal2, b_c - i_c, 0.0), axis=0, keepdims=True)
            logd = (b_c - m_l) - key_term
            pw = (jnp.where(causal2, jnp.exp(logd), 0.0) * scores).astype(BF16)
            local = _dot(pw, jnp.concatenate([stack_heads(v), ones_bd], axis=1))
            b_last = b_c[L - 1:L, :]
            g = b_last - b_c + i_c
            g_max = jnp.max(g, axis=0, keepdims=True)
            kw = (k * jnp.exp(g - g_max)).astype(BF16)
            fresh = _dot_tn(kw, jnp.concatenate([v, jnp.ones_like(v)], axis=1))
            fresh = jnp.where(same_head, fresh, 0.0)

            m_prev = m_state[p]
            carried = _dot(q, c_state[p].astype(BF16))
            m_int = b_c + m_prev
            m_t = jnp.maximum(m_l, m_int)
            a_loc = jnp.exp(m_l - m_t)
            a_int = jnp.exp(m_int - m_t)
            num = a_loc * local[:, :LANES] + a_int * carried[:, :LANES]
            den = a_loc * local[:, LANES:] + a_int * carried[:, LANES:]
            h = num / jnp.maximum(jnp.abs(den), jnp.exp(-m_t))
            o_ref[rows, lanes] = (o_gate[rows, lanes] * h).astype(o_ref.dtype)

            m_new = jnp.maximum(b_last + m_prev, g_max)
            keep = jnp.exp(b_last + m_prev - m_new)
            add = jnp.exp(g_max - m_new)
            c_state[p] = (jnp.concatenate([keep, keep], axis=1) * c_state[p]
                          + jnp.concatenate([add, add], axis=1) * fresh)
            m_state[p] = m_new
            if p == 0 or c < 2:
                m_state[p] = m_new + fill(1)

    for p in range(pairs):
        c_ref[p] = c_state[p]
        m_ref[p] = m_state[p]


def _mixer_in_kernel(x_ref, g_ref, w_ref, lcw_ref, wa_ref, wx_ref, ba_ref, bx_ref, lam_ref, mcw_ref,
                     mbias_ref, lower_ref, ya_ref, yb_ref, sqkv_ref, gates_ref,
                     xg_s, qk_s, vo_s, if_s, lru_pad, lru_h, lru_xa, lru_r, lru_i, m_pad, m_c, m_m,
                     *, tiles_per_seq):
    @pl.when(lax.rem(pl.program_id(0), tiles_per_seq) == 0)
    def _():
        for ref in (lru_pad, lru_h, m_pad, m_c, m_m):
            ref[...] = jnp.zeros_like(ref)

    h = _rmsnorm(x_ref[...], g_ref[...]).astype(BF16)
    pieces = []
    first_col = 0
    for o_ref in (xg_s, qk_s, vo_s, if_s, sqkv_ref, gates_ref):
        for c0 in range(0, o_ref.shape[1], PIECE_COLS):
            c1 = min(c0 + PIECE_COLS, o_ref.shape[1])
            pieces.append((o_ref, c0, c1, first_col + c0))
        first_col += o_ref.shape[1]

    tokens = [jnp.zeros((1, LANES), F32)]

    def fill(n):
        done = pieces[:n]
        del pieces[:n]
        for o_ref, c0, c1, w0 in done:
            res = _dot(h, w_ref[:, w0:w0 + c1 - c0])
            o_ref[:, c0:c1] = res.astype(o_ref.dtype)
            tokens.append(jnp.minimum(jnp.abs(res[0:1, 0:LANES]), 0.0))
        return tokens[-1 - len(done)]

    def fill_through(ref):
        fill(max(n + 1 for n, piece in enumerate(pieces) if piece[0] is ref) if any(
            piece[0] is ref for piece in pieces) else 0)

    fill_through(xg_s)
    _lru_tile(xg_s, lcw_ref, wa_ref, wx_ref, ba_ref, bx_ref, lam_ref, ya_ref, lru_pad, lru_h,
              lru_xa, lru_r, lru_i, fill)
    fill_through(if_s)
    _mlstm_tile(qk_s, vo_s, if_s, mcw_ref, mbias_ref, lower_ref, yb_ref, m_pad, m_c, m_m, fill)
    fill(len(pieces))


def _cumsum_matrix():
    r = np.arange(MLSTM_CUMSUM_ROWS)
    lower = (r[:, None] // MLSTM_CHUNK == r[None, :] // MLSTM_CHUNK) & (r[None, :] <= r[:, None])
    return jnp.asarray(lower, BF16)


def _mixer_in(x, layer_params, l, sb_w, batch):
    T, D = x.shape
    tm = TOKEN_TILE
    w = layer_params[1]
    lru_w = layer_params[2].shape[-1]
    mlstm_w = layer_params[8].shape[-1] // 2
    pairs = mlstm_w // LANES
    lower = _cumsum_matrix()
    tok = lambda i: (i, 0)
    out_widths = (lru_w, mlstm_w, 3 * sb_w, 3 * D)
    assert w.shape[-1] == 2 * lru_w + 4 * mlstm_w + LANES + 3 * sb_w + 3 * D
    return pl.pallas_call(
        functools.partial(_mixer_in_kernel, tiles_per_seq=T // batch // tm), grid=(T // tm,),
        in_specs=[pl.BlockSpec((tm, D), tok)] + [_layer(a, l) for a in layer_params]
                 + [_full(lower.shape)],
        out_specs=[pl.BlockSpec((tm, n), tok) for n in out_widths],
        out_shape=[jax.ShapeDtypeStruct((T, n), BF16) for n in out_widths],
        scratch_shapes=[pltpu.VMEM((tm, 2 * lru_w), F32),
                        pltpu.VMEM((tm, 2 * mlstm_w), F32),
                        pltpu.VMEM((tm, 2 * mlstm_w), BF16),
                        pltpu.VMEM((tm, LANES), F32),
                        pltpu.VMEM((SUBLANES + tm, lru_w), F32),
                        pltpu.VMEM((SUBLANES, lru_w), F32),
                        pltpu.VMEM((tm, lru_w), F32),
                        pltpu.VMEM((tm, lru_w), F32),
                        pltpu.VMEM((tm, lru_w), F32),
                        pltpu.VMEM((SUBLANES + tm, 2 * mlstm_w), F32),
                        pltpu.VMEM((pairs, LANES, 2 * LANES), F32),
                        pltpu.VMEM((pairs, 1, LANES), F32)],
        compiler_params=_params("arbitrary"), name="mixer_in",
    )(x, *layer_params, lower)


def _sb_kernel(q_ref, k_ref, v_ref, o_ref, z_ref, hl_ref, s_ref, w_ref, acc_ref, live_ref):
    tq = q_ref.shape[0]
    nsub = tq // SB_QSUB
    rows2 = 2 * SB_QSUB
    lane = lax.broadcasted_iota(jnp.int32, (1, LANES), 1)
    first_head = lane < HEAD_DIM

    def suffix_ones(n):
        r = lax.broadcasted_iota(jnp.int32, (n, n), 0)
        c = lax.broadcasted_iota(jnp.int32, (n, n), 1)
        return jnp.where(r >= c, 1.0, 0.0).astype(BF16)

    def window(j):
        t0 = pl.program_id(2) * tq + j * SB_QSUB
        start = pl.multiple_of(jnp.maximum(t0 + SB_QSUB - SB_WINDOW, 0), SB_QSUB)
        return t0, start

    def stacked_queries(j):
        q = q_ref[pl.ds(pl.multiple_of(j * SB_QSUB, SB_QSUB), SB_QSUB), :] * (
            1.0 / math.sqrt(HEAD_DIM))
        zero = jnp.zeros_like(q)
        return jnp.concatenate([jnp.where(first_head, q, zero), jnp.where(first_head, zero, q)],
                               axis=0)

    def split_hi_lo(x):
        hi = x.astype(BF16)
        return hi, (x - hi.astype(F32)).astype(BF16)

    def head_select(stacked):
        return jnp.where(first_head, stacked[:SB_QSUB], stacked[SB_QSUB:])

    ones_window = suffix_ones(SB_WINDOW)
    row = lax.broadcasted_iota(jnp.int32, (rows2, SB_WINDOW), 0)
    col = lax.broadcasted_iota(jnp.int32, (rows2, SB_WINDOW), 1)
    key_minus_query = col - jnp.bitwise_and(row, SB_QSUB - 1)

    for j in range(nsub):
        _, start = window(j)
        z_ref[j] = _dot_nt(stacked_queries(j), k_ref[pl.ds(start, SB_WINDOW), :])
    for j in range(nsub):
        t0, start = window(j)
        log_keep = jnp.where(key_minus_query < t0 - start, -_softplus_fast(z_ref[j]), 0.0)
        hi, lo = split_hi_lo(log_keep)
        hl_ref[j, :rows2, :] = hi
        hl_ref[j, rows2:, :] = lo
    for j in range(nsub):
        s_ref[j] = _dot(hl_ref[j], ones_window)
    any_live = None
    for j in range(nsub):
        t0, start = window(j)
        suffix = s_ref[j, :rows2, :] + s_ref[j, rows2:, :]
        weights = jnp.where(key_minus_query < t0 - start, jnp.exp(z_ref[j] + suffix), 0.0)
        w_ref[j] = weights.astype(BF16)
        live = jnp.logical_and(jnp.max(suffix[:, 0:1]) > SB_LOG_WEIGHT_FLOOR, start > 0)
        live_ref[j] = live.astype(jnp.int32)
        any_live = live if any_live is None else jnp.logical_or(any_live, live)
    for j in range(nsub):
        _, start = window(j)
        acc_ref[j * SB_QSUB:(j + 1) * SB_QSUB, :] = head_select(
            _dot(w_ref[j], v_ref[pl.ds(start, SB_WINDOW), :]))

    @pl.when(any_live)
    def _():
        ones_tail = suffix_ones(SB_TAIL)

        def per_group(j, _):
            @pl.when(live_ref[j] != 0)
            def _():
                _, start = window(j)
                q2 = stacked_queries(j)
                out_rows = pl.ds(pl.multiple_of(j * SB_QSUB, SB_QSUB), SB_QSUB)
                carry0 = s_ref[j, :rows2, 0:1] + s_ref[j, rows2:, 0:1]

                def cond(state):
                    pos, carry, _ = state
                    return jnp.logical_and(pos > 0, jnp.max(carry) > SB_LOG_WEIGHT_FLOOR)

                def body(state):
                    pos, carry, acc = state
                    pos = pos - SB_TAIL
                    keys = pl.ds(pl.multiple_of(pos, SB_TAIL), SB_TAIL)
                    z = _dot_nt(q2, k_ref[keys, :])
                    hi, lo = split_hi_lo(-_softplus_fast(z))
                    suffix = _dot(hi, ones_tail) + _dot(lo, ones_tail)
                    weights = jnp.exp(z + suffix + carry).astype(BF16)
                    return (pos, carry + suffix[:, 0:1],
                            acc + head_select(_dot(weights, v_ref[keys, :])))

                _, _, acc = lax.while_loop(cond, body, (start, carry0, acc_ref[out_rows, :]))
                acc_ref[out_rows, :] = acc
            return 0

        lax.fori_loop(0, nsub, per_group, 0)

    o_ref[...] = acc_ref[...].astype(o_ref.dtype)


def _stick_breaking(qkv, batch):
    T, W3 = qkv.shape
    W = W3 // 3
    S = T // batch
    pairs = W // LANES
    tq = SB_QTILE
    nq = S // tq
    nsub = tq // SB_QSUB
    rows2 = 2 * SB_QSUB
    return pl.pallas_call(
        _sb_kernel, grid=(batch, pairs, nq),
        in_specs=[pl.BlockSpec((tq, LANES), lambda b, p, i: (b * nq + i, p)),
                  pl.BlockSpec((S, LANES), lambda b, p, i: (b, pairs + p)),
                  pl.BlockSpec((S, LANES), lambda b, p, i: (b, 2 * pairs + p))],
        out_specs=pl.BlockSpec((tq, LANES), lambda b, p, i: (b * nq + i, p)),
        out_shape=jax.ShapeDtypeStruct((T, W), BF16),
        scratch_shapes=[pltpu.VMEM((nsub, rows2, SB_WINDOW), F32),
                        pltpu.VMEM((nsub, 2 * rows2, SB_WINDOW), BF16),
                        pltpu.VMEM((nsub, 2 * rows2, SB_WINDOW), F32),
                        pltpu.VMEM((nsub, rows2, SB_WINDOW), BF16),
                        pltpu.VMEM((tq, LANES), F32),
                        pltpu.SMEM((nsub,), jnp.int32)],
        compiler_params=_params("parallel", "parallel", "arbitrary"), name="stickbreak",
    )(qkv, qkv, qkv)


def _merge_ffn_kernel(x_ref, ya_ref, yb_ref, yc_ref, gates_ref, wa_ref, wb_ref, wc_ref, wo_ref,
                      gf_ref, wi_ref, wout_ref, gfin_ref, o_ref, act_ref, *, final_norm):
    D = x_ref.shape[1]
    FF = wout_ref.shape[0]
    merged = jnp.zeros(x_ref.shape, F32)
    for idx, (y_ref, w_ref) in enumerate(((ya_ref, wa_ref), (yb_ref, wb_ref), (yc_ref, wc_ref))):
        gate = _sigmoid(gates_ref[:, idx * D:(idx + 1) * D].astype(F32))
        merged = merged + gate * _dot(y_ref[...], w_ref[...])
    x1 = x_ref[...] + _dot(merged.astype(BF16), wo_ref[...])
    hf = _rmsnorm(x1, gf_ref[...]).astype(BF16)
    c0 = 0
    while c0 < FF:
        c1 = min(c0 + DOT_COLS, FF)
        gate = _dot(hf, wi_ref[:, c0:c1])
        up = _dot(hf, wi_ref[:, FF + c0:FF + c1])
        act_ref[:, c0:c1] = (_silu(gate) * up).astype(BF16)
        c0 = c1
    x2 = x1 + _dot(act_ref[...], wout_ref[...])
    if final_norm:
        x2 = _rmsnorm(x2, gfin_ref[...])
    o_ref[...] = x2


def _merge_ffn(x, ya, yb, yc, gates, layer_params, l, gfin, final_norm):
    T, D = x.shape
    tm = TOKEN_TILE
    tok = lambda i: (i, 0)
    ff = layer_params[6].shape[1]
    return pl.pallas_call(
        functools.partial(_merge_ffn_kernel, final_norm=final_norm), grid=(T // tm,),
        in_specs=[pl.BlockSpec((tm, D), tok), pl.BlockSpec((tm, ya.shape[1]), tok),
                  pl.BlockSpec((tm, yb.shape[1]), tok), pl.BlockSpec((tm, yc.shape[1]), tok),
                  pl.BlockSpec((tm, gates.shape[1]), tok)]
                 + [_layer(a, l) for a in layer_params] + [_full(gfin.shape)],
        out_specs=pl.BlockSpec((tm, D), tok),
        out_shape=jax.ShapeDtypeStruct((T, D), F32),
        scratch_shapes=[pltpu.VMEM((tm, ff), BF16)],
        compiler_params=_params("parallel"), name="merge_ffn",
    )(x, ya, yb, yc, gates, *layer_params, gfin)


def _block_diag(w):
    depth, G, n, _ = w.shape
    eye = jnp.eye(G, dtype=w.dtype)
    return (eye[None, :, None, :, None] * w[:, :, :, None, :]).reshape(depth, G * n, G * n)


def kernel(x, norm_mix_g, w_in, conv_lru_w, lru_wa, lru_ba, lru_wx, lru_bx, lru_lambda, conv_mlstm_w, mlstm_ig_b, mlstm_fg_b, w_out_lru, w_out_mlstm, w_out_sb, w_o, norm_ffn_g, w_ffn_in, w_ffn_out, final_norm_g):
    batch, seq, D = x.shape
    depth = w_in.shape[0]
    lru_w = conv_lru_w.shape[-1]
    mlstm_w = conv_mlstm_w.shape[-1] // 2
    heads = mlstm_ig_b.shape[-1]
    sb_w = w_out_sb.shape[1]
    T = batch * seq

    gate_lo = 2 * lru_w + 4 * mlstm_w
    gate_hi = gate_lo + 2 * heads
    assert gate_hi + 3 * sb_w + 3 * D == w_in.shape[-1]
    w_in_b = jnp.concatenate(
        [w_in[..., :gate_hi].astype(BF16),
         jnp.zeros((depth, D, LANES - 2 * heads), BF16),
         w_in[..., gate_hi:].astype(BF16)], axis=-1)
    w_a, w_b, w_c, w_o_b, w_fi, w_fo = (
        w.astype(BF16) for w in (w_out_lru, w_out_mlstm, w_out_sb, w_o, w_ffn_in, w_ffn_out))
    gate_bias = jnp.pad(jnp.concatenate([mlstm_ig_b, mlstm_fg_b], axis=-1).astype(F32),
                        ((0, 0), (0, LANES - 2 * heads)))

    rows = lambda v: v.reshape(depth, 1, -1).astype(F32)
    mixer_params = (rows(norm_mix_g), w_in_b, conv_lru_w.astype(F32),
                    _block_diag(lru_wa).astype(BF16), _block_diag(lru_wx).astype(BF16),
                    rows(lru_ba), rows(lru_bx), rows(lru_lambda),
                    conv_mlstm_w.astype(F32), rows(gate_bias))
    ffn_params = (w_a, w_b, w_c, w_o_b, rows(norm_ffn_g), w_fi, w_fo)
    final_g = final_norm_g.reshape(1, D).astype(F32)

    xt = x.reshape(T, D)
    for l in range(depth):
        ya, yb, s_qkv, gates = _mixer_in(xt, mixer_params, l, sb_w, batch)
        yc = _stick_breaking(s_qkv, batch)
        xt = _merge_ffn(xt, ya, yb, yc, gates, ffn_params, l, final_g,
                        final_norm=(l == depth - 1))
    return xt.reshape(batch, seq, D)
```

```python
import functools
import math

import jax
import jax.numpy as jnp
import numpy as np
from jax import lax
from jax.experimental import pallas as pl
from jax.experimental.pallas import tpu as pltpu

F32 = jnp.float32
BF16 = jnp.bfloat16

LANES = 128
SUBLANES = 8
VMEM_LIMIT_BYTES = 56 * 1024 * 1024

EPS = 1e-6
LRU_C = 8.0
CONV_W = 4
HEAD_DIM = 64
LRU_BLOCK = 16
MLSTM_CHUNK = 64
MLSTM_CUMSUM_ROWS = 256
SB_QSUB = 64
SB_WINDOW = 256
SB_TAIL = 64
SB_LOG_WEIGHT_FLOOR = -110.0

TOKEN_TILE = 512
SB_QTILE = 512
DOT_COLS = 512
PIECE_COLS = 256


def _params(*semantics):
    return pltpu.CompilerParams(dimension_semantics=semantics, vmem_limit_bytes=VMEM_LIMIT_BYTES)


def _full(shape):
    return pl.BlockSpec(shape, lambda *_: (0,) * len(shape))


def _layer(stacked, l):
    zeros = (0,) * (stacked.ndim - 1)
    return pl.BlockSpec((None,) + stacked.shape[1:], lambda *_: (l,) + zeros,
                        pipeline_mode=pl.Buffered(1))


def _rmsnorm(x, g):
    return x * lax.rsqrt(jnp.mean(x * x, axis=-1, keepdims=True) + EPS) * g


def _softplus(x):
    return jnp.maximum(x, 0.0) + jnp.log1p(jnp.exp(-jnp.abs(x)))


def _sigmoid(x):
    return 1.0 / (1.0 + jnp.exp2(x * (-1.0 / math.log(2.0))))


def _silu(x):
    return x * _sigmoid(x)


def _gelu_tanh(x):
    return 0.5 * x * (1.0 + jnp.tanh(math.sqrt(2.0 / math.pi) * (x + 0.044715 * (x * x * x))))


def _one_minus_exp(y):
    poly = jnp.full_like(y, 1.0 / math.factorial(7))
    for n in range(6, 0, -1):
        poly = poly * y + 1.0 / math.factorial(n)
    return jnp.where(y > -0.125, -(y * poly), 1.0 - jnp.exp(y))


def _dot(a, b):
    return jnp.dot(a, b, preferred_element_type=F32)


def _dot_nt(a, b):
    return lax.dot_general(a, b, (((1,), (1,)), ((), ())), preferred_element_type=F32)


def _dot_tn(a, b):
    return lax.dot_general(a, b, (((0,), (0,)), ((), ())), preferred_element_type=F32)


def _split3(x):
    h1 = x.astype(BF16)
    r1 = x - h1.astype(F32)
    h2 = r1.astype(BF16)
    return h1, h2, (r1 - h2.astype(F32)).astype(BF16)


def _dot_01_left(ones, x):
    h1, h2, h3 = _split3(x)
    return _dot(ones, h1) + _dot(ones, h2) + _dot(ones, h3)


def _causal_conv(pad_ref, x, w):
    ts = x.shape[0]
    pad_ref[0:SUBLANES, :] = pad_ref[ts:ts + SUBLANES, :]
    pad_ref[SUBLANES:, :] = x
    out = x * w[CONV_W - 1:CONV_W, :]
    for back in range(1, CONV_W):
        out = out + (pad_ref[SUBLANES - back:SUBLANES - back + ts, :]
                     * w[CONV_W - 1 - back:CONV_W - back, :])
    return out


def _lru_tile(xg_ref, cw_ref, wa_ref, wx_ref, ba_ref, bx_ref, lam_ref, o_ref, pad_ref, h_ref,
              xa_ref, r_ref, i_ref, fill):
    ts, W = o_ref.shape
    xa = _causal_conv(pad_ref, xg_ref[:, :W], cw_ref[...])
    xa_ref[...] = xa
    fill(2)
    xb = xa.astype(BF16)
    r_ref[...] = _dot(xb, wa_ref[...]) + ba_ref[...]
    i_ref[...] = _dot(xb, wx_ref[...]) + bx_ref[...]
    token = fill(2)

    log_decay = (-LRU_C) * _softplus(-lam_ref[...])
    row = lax.broadcasted_iota(jnp.int32, (LRU_BLOCK, W), 0)
    carry = h_ref[SUBLANES - 1:SUBLANES, :] + jnp.concatenate([token] * (W // LANES), axis=1)
    blocks = ts // LRU_BLOCK
    for blk in range(blocks):
        rows = slice(blk * LRU_BLOCK, (blk + 1) * LRU_BLOCK)
        log_a = log_decay * _sigmoid(r_ref[rows, :])
        a = jnp.exp(log_a)
        u = jnp.sqrt(_one_minus_exp(2.0 * log_a)) * (_sigmoid(i_ref[rows, :]) * xa_ref[rows, :])
        span = 1
        while span < LRU_BLOCK:
            keep = row >= span
            a_prev = jnp.where(keep, pltpu.roll(a, span, 0), 1.0)
            u_prev = jnp.where(keep, pltpu.roll(u, span, 0), 0.0)
            u = a * u_prev + u
            a = a * a_prev
            span *= 2
        h = u + a * carry
        carry = h[LRU_BLOCK - 1:LRU_BLOCK, :]
        o_ref[rows, :] = (h * _gelu_tanh(xg_ref[rows, W:])).astype(o_ref.dtype)
        if blk % (blocks // 4) == blocks // 4 - 1:
            token = fill(1)
            carry = carry + jnp.concatenate([token] * (W // LANES), axis=1)
    h_ref[...] = jnp.broadcast_to(carry, h_ref.shape)


def _mlstm_tile(qk_ref, vo_ref, if_ref, cw_ref, bias_ref, lower_ref, o_ref, pad_ref, c_ref, m_ref,
                fill):
    ts, W = o_ref.shape
    L = MLSTM_CHUNK
    pairs = W // LANES
    heads = W // HEAD_DIM

    qk = _silu(_causal_conv(pad_ref, qk_ref[...], cw_ref[...]))
    fill(1)
    q_all = qk[:, :W].astype(BF16)
    k_all = qk[:, W:] * (HEAD_DIM ** -0.5)
    v_all = vo_ref[:, :W]
    o_gate = _sigmoid(vo_ref[:, W:].astype(F32))

    lane = lax.broadcasted_iota(jnp.int32, (1, LANES), 1)
    first_head = lane < HEAD_DIM
    r2 = lax.broadcasted_iota(jnp.int32, (LANES, 2 * LANES), 0)
    c2 = lax.broadcasted_iota(jnp.int32, (LANES, 2 * LANES), 1)
    same_head = (r2 >= HEAD_DIM) == (jnp.bitwise_and(c2, LANES - 1) >= HEAD_DIM)
    ones_bd = jnp.where(same_head[:, :LANES], 1.0, 0.0).astype(BF16)
    rl = lax.broadcasted_iota(jnp.int32, (L, LANES), 0)
    key = jnp.bitwise_and(lax.broadcasted_iota(jnp.int32, (L, LANES), 1), HEAD_DIM - 1)
    causal2 = key <= rl
    diagonal2 = key == rl

    def stack_heads(a):
        zero = jnp.zeros_like(a)
        return jnp.concatenate([jnp.where(first_head, a, zero), jnp.where(first_head, zero, a)],
                               axis=0)

    pre = if_ref[...] + bias_ref[...]
    logf = -_softplus(-pre)
    pos = jnp.bitwise_and(lax.broadcasted_iota(jnp.int32, (ts, LANES), 0), L - 1)
    i_rep, b_rep, m_loc = [], [], []
    for p in range(pairs):
        i_p = jnp.where(first_head, pre[:, 2 * p:2 * p + 1], pre[:, 2 * p + 1:2 * p + 2])
        f_p = jnp.where(first_head, logf[:, heads + 2 * p:heads + 2 * p + 1],
                        logf[:, heads + 2 * p + 1:heads + 2 * p + 2])
        b_p = jnp.concatenate(
            [_dot_01_left(lower_ref[...], f_p[r0:r0 + MLSTM_CUMSUM_ROWS])
             for r0 in range(0, ts, MLSTM_CUMSUM_ROWS)], axis=0)
        run = i_p - b_p
        span = 1
        while span < L:
            run = jnp.where(pos >= span, jnp.maximum(run, pltpu.roll(run, span, 0)), run)
            span *= 2
        fill(1)
        i_rep.append(i_p)
        b_rep.append(b_p)
        m_loc.append(b_p + run)

    m_state = [m_ref[p] for p in range(pairs)]
    c_state = [c_ref[p] for p in range(pairs)]

    for c in range(ts // L):
        rows = slice(c * L, (c + 1) * L)
        for p in range(pairs):
            lanes = slice(p * LANES, (p + 1) * LANES)
            q = q_all[rows, lanes]
            k = k_all[rows, lanes]
            v = v_all[rows, lanes]
            b_c = b_rep[p][rows]
            i_c = i_rep[p][rows]
            m_l = m_loc[p][rows]

            scores = _dot_nt(q, stack_heads(k.astype(BF16)))
            key_term = jnp.sum(jnp.where(diagonal2, b_c - i_c, 0.0), axis=0, keepdims=True)
            logd = (b_c - m_l) - key_term
            pw = (jnp.where(causal2, jnp.exp(logd), 0.0) * scores).astype(BF16)
            local = _dot(pw, jnp.concatenate([stack_heads(v), ones_bd], axis=1))
            b_last = b_c[L - 1:L, :]
            g = b_last - b_c + i_c
            g_max = jnp.max(g, axis=0, keepdims=True)
            kw = (k * jnp.exp(g - g_max)).astype(BF16)
            fresh = _dot_tn(kw, jnp.concatenate([v, jnp.ones_like(v)], axis=1))
            fresh = jnp.where(same_head, fresh, 0.0)

            m_prev = m_state[p]
            carried = _dot(q, c_state[p].astype(BF16))
            m_int = b_c + m_prev
            m_t = jnp.maximum(m_l, m_int)
            a_loc = jnp.exp(m_l - m_t)
            a_int = jnp.exp(m_int - m_t)
            num = a_loc * local[:, :LANES] + a_int * carried[:, :LANES]
            den = a_loc * local[:, LANES:] + a_int * carried[:, LANES:]
            h = num / jnp.maximum(jnp.abs(den), jnp.exp(-m_t))
            o_ref[rows, lanes] = (o_gate[rows, lanes] * h).astype(o_ref.dtype)

            m_new = jnp.maximum(b_last + m_prev, g_max)
            keep = jnp.exp(b_last + m_prev - m_new)
            add = jnp.exp(g_max - m_new)
            c_state[p] = (jnp.concatenate([keep, keep], axis=1) * c_state[p]
                          + jnp.concatenate([add, add], axis=1) * fresh)
            m_state[p] = m_new
            if p == 0 or c < 2:
                m_state[p] = m_new + fill(1)

    for p in range(pairs):
        c_ref[p] = c_state[p]
        m_ref[p] = m_state[p]


def _mixer_in_kernel(x_ref, g_ref, w_ref, lcw_ref, wa_ref, wx_ref, ba_ref, bx_ref, lam_ref, mcw_ref,
                     mbias_ref, lower_ref, ya_ref, yb_ref, sqkv_ref, gates_ref,
                     xg_s, qk_s, vo_s, if_s, lru_pad, lru_h, lru_xa, lru_r, lru_i, m_pad, m_c, m_m,
                     *, tiles_per_seq):
    @pl.when(lax.rem(pl.program_id(0), tiles_per_seq) == 0)
    def _():
        for ref in (lru_pad, lru_h, m_pad, m_c, m_m):
            ref[...] = jnp.zeros_like(ref)

    h = _rmsnorm(x_ref[...], g_ref[...]).astype(BF16)
    pieces = []
    first_col = 0
    for o_ref in (xg_s, qk_s, vo_s, if_s, sqkv_ref, gates_ref):
        for c0 in range(0, o_ref.shape[1], PIECE_COLS):
            c1 = min(c0 + PIECE_COLS, o_ref.shape[1])
            pieces.append((o_ref, c0, c1, first_col + c0))
        first_col += o_ref.shape[1]

    tokens = [jnp.zeros((1, LANES), F32)]

    def fill(n):
        done = pieces[:n]
        del pieces[:n]
        for o_ref, c0, c1, w0 in done:
            res = _dot(h, w_ref[:, w0:w0 + c1 - c0])
            o_ref[:, c0:c1] = res.astype(o_ref.dtype)
            tokens.append(jnp.minimum(jnp.abs(res[0:1, 0:LANES]), 0.0))
        return tokens[-1 - len(done)]

    def fill_through(ref):
        fill(max(n + 1 for n, piece in enumerate(pieces) if piece[0] is ref) if any(
            piece[0] is ref for piece in pieces) else 0)

    fill_through(xg_s)
    _lru_tile(xg_s, lcw_ref, wa_ref, wx_ref, ba_ref, bx_ref, lam_ref, ya_ref, lru_pad, lru_h,
              lru_xa, lru_r, lru_i, fill)
    fill_through(if_s)
    _mlstm_tile(qk_s, vo_s, if_s, mcw_ref, mbias_ref, lower_ref, yb_ref, m_pad, m_c, m_m, fill)
    fill(len(pieces))


def _cumsum_matrix():
    r = np.arange(MLSTM_CUMSUM_ROWS)
    lower = (r[:, None] // MLSTM_CHUNK == r[None, :] // MLSTM_CHUNK) & (r[None, :] <= r[:, None])
    return jnp.asarray(lower, BF16)


def _mixer_in(x, layer_params, l, sb_w, batch):
    T, D = x.shape
    tm = TOKEN_TILE
    w = layer_params[1]
    lru_w = layer_params[2].shape[-1]
    mlstm_w = layer_params[8].shape[-1] // 2
    pairs = mlstm_w // LANES
    lower = _cumsum_matrix()
    tok = lambda i: (i, 0)
    out_widths = (lru_w, mlstm_w, 3 * sb_w, 3 * D)
    assert w.shape[-1] == 2 * lru_w + 4 * mlstm_w + LANES + 3 * sb_w + 3 * D
    return pl.pallas_call(
        functools.partial(_mixer_in_kernel, tiles_per_seq=T // batch // tm), grid=(T // tm,),
        in_specs=[pl.BlockSpec((tm, D), tok)] + [_layer(a, l) for a in layer_params]
                 + [_full(lower.shape)],
        out_specs=[pl.BlockSpec((tm, n), tok) for n in out_widths],
        out_shape=[jax.ShapeDtypeStruct((T, n), BF16) for n in out_widths],
        scratch_shapes=[pltpu.VMEM((tm, 2 * lru_w), F32),
                        pltpu.VMEM((tm, 2 * mlstm_w), F32),
                        pltpu.VMEM((tm, 2 * mlstm_w), BF16),
                        pltpu.VMEM((tm, LANES), F32),
                        pltpu.VMEM((SUBLANES + tm, lru_w), F32),
                        pltpu.VMEM((SUBLANES, lru_w), F32),
                        pltpu.VMEM((tm, lru_w), F32),
                        pltpu.VMEM((tm, lru_w), F32),
                        pltpu.VMEM((tm, lru_w), F32),
                        pltpu.VMEM((SUBLANES + tm, 2 * mlstm_w), F32),
                        pltpu.VMEM((pairs, LANES, 2 * LANES), F32),
                        pltpu.VMEM((pairs, 1, LANES), F32)],
        compiler_params=_params("arbitrary"), name="mixer_in",
    )(x, *layer_params, lower)


def _sb_kernel(q_ref, k_ref, v_ref, o_ref, z_ref, hl_ref, s_ref, w_ref, acc_ref, live_ref):
    tq = q_ref.shape[0]
    nsub = tq // SB_QSUB
    rows2 = 2 * SB_QSUB
    lookback = SB_WINDOW - SB_QSUB
    lane = lax.broadcasted_iota(jnp.int32, (1, LANES), 1)
    first_head = lane < HEAD_DIM

    def minus_suffix_ones(n):
        r = jnp.bitwise_and(lax.broadcasted_iota(jnp.int32, (2 * n, n), 0), n - 1)
        c = lax.broadcasted_iota(jnp.int32, (2 * n, n), 1)
        return jnp.where(r >= c, -1.0, 0.0).astype(BF16)

    def window(j):
        t0 = pl.program_id(2) * tq + j * SB_QSUB
        start = pl.multiple_of(jnp.maximum(t0 - lookback, 0), SB_QSUB)
        return t0, start

    def stacked_queries(j):
        q = q_ref[pl.ds(pl.multiple_of(j * SB_QSUB, SB_QSUB), SB_QSUB), :] * (
            1.0 / math.sqrt(HEAD_DIM))
        zero = jnp.zeros_like(q)
        return jnp.concatenate([jnp.where(first_head, q, zero), jnp.where(first_head, zero, q)],
                               axis=0)

    def softplus(z):
        return jnp.maximum(z, 0.0) + jnp.log(1.0 + jnp.exp2(jnp.abs(z) * (-1.0 / math.log(2.0))))

    def hi_lo(x):
        hi = x.astype(BF16)
        return jnp.concatenate([hi, (x - hi.astype(F32)).astype(BF16)], axis=1)

    def head_select(stacked):
        return jnp.where(first_head, stacked[:SB_QSUB], stacked[SB_QSUB:])

    row = lax.broadcasted_iota(jnp.int32, (rows2, SB_WINDOW), 0)
    col = lax.broadcasted_iota(jnp.int32, (rows2, SB_WINDOW), 1)
    key_minus_query = col - jnp.bitwise_and(row, SB_QSUB - 1)

    def window_pass(clamped):
        def mask(j, x):
            aligned = lookback // LANES * LANES
            if clamped or aligned == 0:
                t0, start = window(j)
                return jnp.where(key_minus_query < t0 - start, x, 0.0)
            return jnp.concatenate(
                [x[:, :aligned],
                 jnp.where(key_minus_query[:, aligned:] < lookback, x[:, aligned:], 0.0)], axis=1)

        minus_ones = minus_suffix_ones(SB_WINDOW)
        for j in range(nsub):
            _, start = window(j)
            z_ref[j] = _dot_nt(stacked_queries(j), k_ref[pl.ds(start, SB_WINDOW), :])
        for j in range(nsub):
            hl_ref[j] = hi_lo(mask(j, softplus(z_ref[j])))
        for j in range(nsub):
            s_ref[j] = _dot(hl_ref[j], minus_ones)
        any_live = None
        for j in range(nsub):
            _, start = window(j)
            suffix = s_ref[j]
            w_ref[j] = mask(j, jnp.exp(z_ref[j] + suffix)).astype(BF16)
            live = jnp.logical_and(jnp.max(suffix[:, 0:1]) > SB_LOG_WEIGHT_FLOOR, start > 0)
            live_ref[j] = live.astype(jnp.int32)
            any_live = live if any_live is None else jnp.logical_or(any_live, live)
        live_ref[nsub] = any_live.astype(jnp.int32)
        for j in range(nsub):
            _, start = window(j)
            acc_ref[j * SB_QSUB:(j + 1) * SB_QSUB, :] = head_select(
                _dot(w_ref[j], v_ref[pl.ds(start, SB_WINDOW), :]))

    first_step = pl.program_id(2) == 0
    pl.when(first_step)(functools.partial(window_pass, True))
    pl.when(jnp.logical_not(first_step))(functools.partial(window_pass, False))

    @pl.when(live_ref[nsub] != 0)
    def _():
        minus_ones = minus_suffix_ones(SB_TAIL)

        def per_group(j, _):
            @pl.when(live_ref[j] != 0)
            def _():
                _, start = window(j)
                q2 = stacked_queries(j)
                out_rows = pl.ds(pl.multiple_of(j * SB_QSUB, SB_QSUB), SB_QSUB)

                def cond(state):
                    pos, carry, _ = state
                    return jnp.logical_and(pos > 0, jnp.max(carry) > SB_LOG_WEIGHT_FLOOR)

                def body(state):
                    pos, carry, acc = state
                    pos = pos - SB_TAIL
                    keys = pl.ds(pl.multiple_of(pos, SB_TAIL), SB_TAIL)
                    z = _dot_nt(q2, k_ref[keys, :])
                    suffix = _dot(hi_lo(softplus(z)), minus_ones)
                    weights = jnp.exp(z + suffix + carry).astype(BF16)
                    return (pos, carry + suffix[:, 0:1],
                            acc + head_select(_dot(weights, v_ref[keys, :])))

                _, _, acc = lax.while_loop(
                    cond, body, (start, s_ref[j, :, 0:1], acc_ref[out_rows, :]))
                acc_ref[out_rows, :] = acc
            return 0

        lax.fori_loop(0, nsub, per_group, 0)

    o_ref[...] = acc_ref[...].astype(o_ref.dtype)


def _stick_breaking(qkv, batch):
    T, W3 = qkv.shape
    W = W3 // 3
    S = T // batch
    pairs = W // LANES
    tq = SB_QTILE
    nq = S // tq
    nsub = tq // SB_QSUB
    rows2 = 2 * SB_QSUB
    return pl.pallas_call(
        _sb_kernel, grid=(batch, pairs, nq),
        in_specs=[pl.BlockSpec((tq, LANES), lambda b, p, i: (b * nq + i, p)),
                  pl.BlockSpec((S, LANES), lambda b, p, i: (b, pairs + p)),
                  pl.BlockSpec((S, LANES), lambda b, p, i: (b, 2 * pairs + p))],
        out_specs=pl.BlockSpec((tq, LANES), lambda b, p, i: (b * nq + i, p)),
        out_shape=jax.ShapeDtypeStruct((T, W), BF16),
        scratch_shapes=[pltpu.VMEM((nsub, rows2, SB_WINDOW), F32),
                        pltpu.VMEM((nsub, rows2, 2 * SB_WINDOW), BF16),
                        pltpu.VMEM((nsub, rows2, SB_WINDOW), F32),
                        pltpu.VMEM((nsub, rows2, SB_WINDOW), BF16),
                        pltpu.VMEM((tq, LANES), F32),
                        pltpu.SMEM((nsub + 1,), jnp.int32)],
        compiler_params=_params("parallel", "parallel", "arbitrary"), name="stickbreak",
    )(qkv, qkv, qkv)


def _merge_ffn_kernel(x_ref, ya_ref, yb_ref, yc_ref, gates_ref, wa_ref, wb_ref, wc_ref, wo_ref,
                      gf_ref, wi_ref, wout_ref, gfin_ref, o_ref, act_ref, *, final_norm):
    D = x_ref.shape[1]
    FF = wout_ref.shape[0]
    merged = jnp.zeros(x_ref.shape, F32)
    for idx, (y_ref, w_ref) in enumerate(((ya_ref, wa_ref), (yb_ref, wb_ref), (yc_ref, wc_ref))):
        gate = _sigmoid(gates_ref[:, idx * D:(idx + 1) * D].astype(F32))
        merged = merged + gate * _dot(y_ref[...], w_ref[...])
    x1 = x_ref[...] + _dot(merged.astype(BF16), wo_ref[...])
    hf = _rmsnorm(x1, gf_ref[...]).astype(BF16)
    c0 = 0
    while c0 < FF:
        c1 = min(c0 + DOT_COLS, FF)
        gate = _dot(hf, wi_ref[:, c0:c1])
        up = _dot(hf, wi_ref[:, FF + c0:FF + c1])
        act_ref[:, c0:c1] = (_silu(gate) * up).astype(BF16)
        c0 = c1
    x2 = x1 + _dot(act_ref[...], wout_ref[...])
    if final_norm:
        x2 = _rmsnorm(x2, gfin_ref[...])
    o_ref[...] = x2


def _merge_ffn(x, ya, yb, yc, gates, layer_params, l, gfin, final_norm):
    T, D = x.shape
    tm = TOKEN_TILE
    tok = lambda i: (i, 0)
    ff = layer_params[6].shape[1]
    return pl.pallas_call(
        functools.partial(_merge_ffn_kernel, final_norm=final_norm), grid=(T // tm,),
        in_specs=[pl.BlockSpec((tm, D), tok), pl.BlockSpec((tm, ya.shape[1]), tok),
                  pl.BlockSpec((tm, yb.shape[1]), tok), pl.BlockSpec((tm, yc.shape[1]), tok),
                  pl.BlockSpec((tm, gates.shape[1]), tok)]
                 + [_layer(a, l) for a in layer_params] + [_full(gfin.shape)],
        out_specs=pl.BlockSpec((tm, D), tok),
        out_shape=jax.ShapeDtypeStruct((T, D), F32),
        scratch_shapes=[pltpu.VMEM((tm, ff), BF16)],
        compiler_params=_params("parallel"), name="merge_ffn",
    )(x, ya, yb, yc, gates, *layer_params, gfin)


def _block_diag(w):
    depth, G, n, _ = w.shape
    eye = jnp.eye(G, dtype=w.dtype)
    return (eye[None, :, None, :, None] * w[:, :, :, None, :]).reshape(depth, G * n, G * n)


def kernel(x, norm_mix_g, w_in, conv_lru_w, lru_wa, lru_ba, lru_wx, lru_bx, lru_lambda, conv_mlstm_w, mlstm_ig_b, mlstm_fg_b, w_out_lru, w_out_mlstm, w_out_sb, w_o, norm_ffn_g, w_ffn_in, w_ffn_out, final_norm_g):
    batch, seq, D = x.shape
    depth = w_in.shape[0]
    lru_w = conv_lru_w.shape[-1]
    mlstm_w = conv_mlstm_w.shape[-1] // 2
    heads = mlstm_ig_b.shape[-1]
    sb_w = w_out_sb.shape[1]
    T = batch * seq

    gate_lo = 2 * lru_w + 4 * mlstm_w
    gate_hi = gate_lo + 2 * heads
    assert gate_hi + 3 * sb_w + 3 * D == w_in.shape[-1]
    w_in_b = w_in.astype(BF16)
    w_in_b = jnp.concatenate(
        [w_in_b[..., :gate_hi], jnp.zeros((depth, D, LANES - 2 * heads), BF16),
         w_in_b[..., gate_hi:]], axis=-1)
    w_a, w_b, w_c, w_o_b, w_fi, w_fo = (
        w.astype(BF16) for w in (w_out_lru, w_out_mlstm, w_out_sb, w_o, w_ffn_in, w_ffn_out))
    gate_bias = jnp.pad(jnp.concatenate([mlstm_ig_b, mlstm_fg_b], axis=-1).astype(F32),
                        ((0, 0), (0, LANES - 2 * heads)))

    rows = lambda v: v.reshape(depth, 1, -1).astype(F32)
    mixer_params = (rows(norm_mix_g), w_in_b, conv_lru_w.astype(F32),
                    _block_diag(lru_wa).astype(BF16), _block_diag(lru_wx).astype(BF16),
                    rows(lru_ba), rows(lru_bx), rows(lru_lambda),
                    conv_mlstm_w.astype(F32), rows(gate_bias))
    ffn_params = (w_a, w_b, w_c, w_o_b, rows(norm_ffn_g), w_fi, w_fo)
    final_g = final_norm_g.reshape(1, D).astype(F32)

    xt = x.reshape(T, D)
    for l in range(depth):
        ya, yb, s_qkv, gates = _mixer_in(xt, mixer_params, l, sb_w, batch)
        yc = _stick_breaking(s_qkv, batch)
        xt = _merge_ffn(xt, ya, yb, yc, gates, ffn_params, l, final_g,
                        final_norm=(l == depth - 1))
    return xt.reshape(batch, seq, D)
```

```python
import functools
import math

import jax
import jax.numpy as jnp
import numpy as np
from jax import lax
from jax.experimental import pallas as pl
from jax.experimental.pallas import tpu as pltpu

F32 = jnp.float32
BF16 = jnp.bfloat16

LANES = 128
SUBLANES = 8
VMEM_LIMIT_BYTES = 56 * 1024 * 1024

EPS = 1e-6
LRU_C = 8.0
CONV_W = 4
HEAD_DIM = 64
LRU_BLOCK = 16
MLSTM_CHUNK = 64
MLSTM_CUMSUM_ROWS = 256
SB_QSUB = 64
SB_WINDOW = 256
SB_TAIL = 64
SB_LOG_WEIGHT_FLOOR = -110.0

TOKEN_TILE = 512
SB_QTILE = 1024
DOT_COLS = 512
PIECE_COLS = 256


def _params(*semantics):
    return pltpu.CompilerParams(dimension_semantics=semantics, vmem_limit_bytes=VMEM_LIMIT_BYTES)


def _full(shape):
    return pl.BlockSpec(shape, lambda *_: (0,) * len(shape))


def _layer(stacked, l):
    zeros = (0,) * (stacked.ndim - 1)
    return pl.BlockSpec((None,) + stacked.shape[1:], lambda *_: (l,) + zeros,
                        pipeline_mode=pl.Buffered(1))


def _rmsnorm(x, g):
    return x * lax.rsqrt(jnp.mean(x * x, axis=-1, keepdims=True) + EPS) * g


def _softplus(x):
    return jnp.maximum(x, 0.0) + jnp.log1p(jnp.exp(-jnp.abs(x)))


def _softplus_log_domain(x):
    return jnp.maximum(x, 0.0) + jnp.log(1.0 + jnp.exp2(jnp.abs(x) * (-1.0 / math.log(2.0))))


def _sigmoid(x):
    return 1.0 / (1.0 + jnp.exp2(x * (-1.0 / math.log(2.0))))


def _silu(x):
    return x * _sigmoid(x)


def _gelu_tanh(x):
    return 0.5 * x * (1.0 + jnp.tanh(math.sqrt(2.0 / math.pi) * (x + 0.044715 * (x * x * x))))


def _one_minus_exp(y):
    poly = jnp.full_like(y, 1.0 / math.factorial(7))
    for n in range(6, 0, -1):
        poly = poly * y + 1.0 / math.factorial(n)
    return jnp.where(y > -0.125, -(y * poly), 1.0 - jnp.exp(y))


def _dot(a, b):
    return jnp.dot(a, b, preferred_element_type=F32)


def _dot_nt(a, b):
    return lax.dot_general(a, b, (((1,), (1,)), ((), ())), preferred_element_type=F32)


def _dot_tn(a, b):
    return lax.dot_general(a, b, (((0,), (0,)), ((), ())), preferred_element_type=F32)


def _split3(x):
    h1 = x.astype(BF16)
    r1 = x - h1.astype(F32)
    h2 = r1.astype(BF16)
    return h1, h2, (r1 - h2.astype(F32)).astype(BF16)


def _dot_01_left(ones, x):
    h1, h2, h3 = _split3(x)
    return _dot(ones, h1) + _dot(ones, h2) + _dot(ones, h3)


def _causal_conv(pad_ref, x, w):
    ts = x.shape[0]
    pad_ref[0:SUBLANES, :] = pad_ref[ts:ts + SUBLANES, :]
    pad_ref[SUBLANES:, :] = x
    out = x * w[CONV_W - 1:CONV_W, :]
    for back in range(1, CONV_W):
        out = out + (pad_ref[SUBLANES - back:SUBLANES - back + ts, :]
                     * w[CONV_W - 1 - back:CONV_W - back, :])
    return out


def _lru_tile(xg_ref, cw_ref, wa_ref, wx_ref, ba_ref, bx_ref, lam_ref, o_ref, pad_ref, h_ref,
              xa_ref, r_ref, i_ref, fill):
    ts, W = o_ref.shape
    xa = _causal_conv(pad_ref, xg_ref[:, :W], cw_ref[...])
    xa_ref[...] = xa
    fill(2)
    xb = xa.astype(BF16)
    r_ref[...] = _dot(xb, wa_ref[...]) + ba_ref[...]
    i_ref[...] = _dot(xb, wx_ref[...]) + bx_ref[...]
    token = fill(2)

    log_decay = (-LRU_C) * _softplus(-lam_ref[...])
    row = lax.broadcasted_iota(jnp.int32, (LRU_BLOCK, W), 0)
    carry = h_ref[SUBLANES - 1:SUBLANES, :] + jnp.concatenate([token] * (W // LANES), axis=1)
    blocks = ts // LRU_BLOCK
    for blk in range(blocks):
        rows = slice(blk * LRU_BLOCK, (blk + 1) * LRU_BLOCK)
        log_a = log_decay * _sigmoid(r_ref[rows, :])
        a = jnp.exp(log_a)
        u = jnp.sqrt(_one_minus_exp(2.0 * log_a)) * (_sigmoid(i_ref[rows, :]) * xa_ref[rows, :])
        span = 1
        while span < LRU_BLOCK:
            keep = row >= span
            a_prev = jnp.where(keep, pltpu.roll(a, span, 0), 1.0)
            u_prev = jnp.where(keep, pltpu.roll(u, span, 0), 0.0)
            u = a * u_prev + u
            a = a * a_prev
            span *= 2
        h = u + a * carry
        carry = h[LRU_BLOCK - 1:LRU_BLOCK, :]
        o_ref[rows, :] = (h * _gelu_tanh(xg_ref[rows, W:])).astype(o_ref.dtype)
        if blk % (blocks // 4) == blocks // 4 - 1:
            token = fill(1)
            carry = carry + jnp.concatenate([token] * (W // LANES), axis=1)
    h_ref[...] = jnp.broadcast_to(carry, h_ref.shape)


def _mlstm_tile(qk_ref, vo_ref, if_ref, cw_ref, bias_ref, lower_ref, o_ref, pad_ref, c_ref, m_ref,
                fill):
    ts, W = o_ref.shape
    L = MLSTM_CHUNK
    pairs = W // LANES
    heads = W // HEAD_DIM

    qk = _silu(_causal_conv(pad_ref, qk_ref[...], cw_ref[...]))
    fill(1)
    q_all = qk[:, :W].astype(BF16)
    k_all = qk[:, W:] * (HEAD_DIM ** -0.5)
    v_all = vo_ref[:, :W]
    o_gate = _sigmoid(vo_ref[:, W:].astype(F32))

    lane = lax.broadcasted_iota(jnp.int32, (1, LANES), 1)
    first_head = lane < HEAD_DIM
    r2 = lax.broadcasted_iota(jnp.int32, (LANES, 2 * LANES), 0)
    c2 = lax.broadcasted_iota(jnp.int32, (LANES, 2 * LANES), 1)
    same_head = (r2 >= HEAD_DIM) == (jnp.bitwise_and(c2, LANES - 1) >= HEAD_DIM)
    ones_bd = jnp.where(same_head[:, :LANES], 1.0, 0.0).astype(BF16)
    rl = lax.broadcasted_iota(jnp.int32, (L, LANES), 0)
    key = jnp.bitwise_and(lax.broadcasted_iota(jnp.int32, (L, LANES), 1), HEAD_DIM - 1)
    causal2 = key <= rl
    diagonal2 = key == rl

    def stack_heads(a):
        zero = jnp.zeros_like(a)
        return jnp.concatenate([jnp.where(first_head, a, zero), jnp.where(first_head, zero, a)],
                               axis=0)

    pre = if_ref[...] + bias_ref[...]
    logf = -_softplus_log_domain(-pre)
    pos = jnp.bitwise_and(lax.broadcasted_iota(jnp.int32, (ts, LANES), 0), L - 1)
    i_rep, b_rep, m_loc = [], [], []
    for p in range(pairs):
        i_p = jnp.where(first_head, pre[:, 2 * p:2 * p + 1], pre[:, 2 * p + 1:2 * p + 2])
        f_p = jnp.where(first_head, logf[:, heads + 2 * p:heads + 2 * p + 1],
                        logf[:, heads + 2 * p + 1:heads + 2 * p + 2])
        b_p = jnp.concatenate(
            [_dot_01_left(lower_ref[...], f_p[r0:r0 + MLSTM_CUMSUM_ROWS])
             for r0 in range(0, ts, MLSTM_CUMSUM_ROWS)], axis=0)
        run = i_p - b_p
        span = 1
        while span < L:
            run = jnp.where(pos >= span, jnp.maximum(run, pltpu.roll(run, span, 0)), run)
            span *= 2
        fill(1)
        i_rep.append(i_p)
        b_rep.append(b_p)
        m_loc.append(b_p + run)

    m_state = [m_ref[p] for p in range(pairs)]
    c_state = [c_ref[p] for p in range(pairs)]

    for c in range(ts // L):
        rows = slice(c * L, (c + 1) * L)
        for p in range(pairs):
            lanes = slice(p * LANES, (p + 1) * LANES)
            q = q_all[rows, lanes]
            k = k_all[rows, lanes]
            v = v_all[rows, lanes]
            b_c = b_rep[p][rows]
            i_c = i_rep[p][rows]
            m_l = m_loc[p][rows]

            scores = _dot_nt(q, stack_heads(k.astype(BF16)))
            key_term = jnp.sum(jnp.where(diagonal2, b_c - i_c, 0.0), axis=0, keepdims=True)
            logd = (b_c - m_l) - key_term
            pw = (jnp.where(causal2, jnp.exp(logd), 0.0) * scores).astype(BF16)
            local = _dot(pw, jnp.concatenate([stack_heads(v), ones_bd], axis=1))
            b_last = b_c[L - 1:L, :]
            g = b_last - b_c + i_c
            g_max = jnp.max(g, axis=0, keepdims=True)
            kw = (k * jnp.exp(g - g_max)).astype(BF16)
            fresh = _dot_tn(kw, jnp.concatenate([v, jnp.ones_like(v)], axis=1))
            fresh = jnp.where(same_head, fresh, 0.0)

            m_prev = m_state[p]
            carried = _dot(q, c_state[p].astype(BF16))
            m_int = b_c + m_prev
            m_t = jnp.maximum(m_l, m_int)
            a_loc = jnp.exp(m_l - m_t)
            a_int = jnp.exp(m_int - m_t)
            num = a_loc * local[:, :LANES] + a_int * carried[:, :LANES]
            den = a_loc * local[:, LANES:] + a_int * carried[:, LANES:]
            h = num / jnp.maximum(jnp.abs(den), jnp.exp(-m_t))
            o_ref[rows, lanes] = (o_gate[rows, lanes] * h).astype(o_ref.dtype)

            m_new = jnp.maximum(b_last + m_prev, g_max)
            keep = jnp.exp(b_last + m_prev - m_new)
            add = jnp.exp(g_max - m_new)
            c_state[p] = (jnp.concatenate([keep, keep], axis=1) * c_state[p]
                          + jnp.concatenate([add, add], axis=1) * fresh)
            m_state[p] = m_new
            if p == 0 or c < 2:
                m_state[p] = m_new + fill(1)

    for p in range(pairs):
        c_ref[p] = c_state[p]
        m_ref[p] = m_state[p]


def _mixer_in_kernel(x_ref, g_ref, w_ref, lcw_ref, wa_ref, wx_ref, ba_ref, bx_ref, lam_ref, mcw_ref,
                     mbias_ref, lower_ref, ya_ref, yb_ref, sqkv_ref, gates_ref,
                     xg_s, qk_s, vo_s, if_s, lru_pad, lru_h, lru_xa, lru_r, lru_i, m_pad, m_c, m_m,
                     *, tiles_per_seq):
    @pl.when(lax.rem(pl.program_id(0), tiles_per_seq) == 0)
    def _():
        for ref in (lru_pad, lru_h, m_pad, m_c, m_m):
            ref[...] = jnp.zeros_like(ref)

    h = _rmsnorm(x_ref[...], g_ref[...]).astype(BF16)
    pieces = []
    first_col = 0
    for o_ref in (xg_s, qk_s, vo_s, if_s, sqkv_ref, gates_ref):
        for c0 in range(0, o_ref.shape[1], PIECE_COLS):
            c1 = min(c0 + PIECE_COLS, o_ref.shape[1])
            pieces.append((o_ref, c0, c1, first_col + c0))
        first_col += o_ref.shape[1]

    tokens = [jnp.zeros((1, LANES), F32)]

    def fill(n):
        done = pieces[:n]
        del pieces[:n]
        for o_ref, c0, c1, w0 in done:
            res = _dot(h, w_ref[:, w0:w0 + c1 - c0])
            o_ref[:, c0:c1] = res.astype(o_ref.dtype)
            tokens.append(jnp.minimum(jnp.abs(res[0:1, 0:LANES]), 0.0))
        return tokens[-1 - len(done)]

    def fill_through(ref):
        fill(max(n + 1 for n, piece in enumerate(pieces) if piece[0] is ref) if any(
            piece[0] is ref for piece in pieces) else 0)

    fill_through(xg_s)
    _lru_tile(xg_s, lcw_ref, wa_ref, wx_ref, ba_ref, bx_ref, lam_ref, ya_ref, lru_pad, lru_h,
              lru_xa, lru_r, lru_i, fill)
    fill_through(if_s)
    _mlstm_tile(qk_s, vo_s, if_s, mcw_ref, mbias_ref, lower_ref, yb_ref, m_pad, m_c, m_m, fill)
    fill(len(pieces))


def _cumsum_matrix():
    r = np.arange(MLSTM_CUMSUM_ROWS)
    lower = (r[:, None] // MLSTM_CHUNK == r[None, :] // MLSTM_CHUNK) & (r[None, :] <= r[:, None])
    return jnp.asarray(lower, BF16)


def _mixer_in(x, layer_params, l, sb_w, batch):
    T, D = x.shape
    tm = TOKEN_TILE
    w = layer_params[1]
    lru_w = layer_params[2].shape[-1]
    mlstm_w = layer_params[8].shape[-1] // 2
    pairs = mlstm_w // LANES
    lower = _cumsum_matrix()
    tok = lambda i: (i, 0)
    out_widths = (lru_w, mlstm_w, 3 * sb_w, 3 * D)
    assert w.shape[-1] == 2 * lru_w + 4 * mlstm_w + LANES + 3 * sb_w + 3 * D
    return pl.pallas_call(
        functools.partial(_mixer_in_kernel, tiles_per_seq=T // batch // tm), grid=(T // tm,),
        in_specs=[pl.BlockSpec((tm, D), tok)] + [_layer(a, l) for a in layer_params]
                 + [_full(lower.shape)],
        out_specs=[pl.BlockSpec((tm, n), tok) for n in out_widths],
        out_shape=[jax.ShapeDtypeStruct((T, n), BF16) for n in out_widths],
        scratch_shapes=[pltpu.VMEM((tm, 2 * lru_w), F32),
                        pltpu.VMEM((tm, 2 * mlstm_w), F32),
                        pltpu.VMEM((tm, 2 * mlstm_w), BF16),
                        pltpu.VMEM((tm, LANES), F32),
                        pltpu.VMEM((SUBLANES + tm, lru_w), F32),
                        pltpu.VMEM((SUBLANES, lru_w), F32),
                        pltpu.VMEM((tm, lru_w), F32),
                        pltpu.VMEM((tm, lru_w), F32),
                        pltpu.VMEM((tm, lru_w), F32),
                        pltpu.VMEM((SUBLANES + tm, 2 * mlstm_w), F32),
                        pltpu.VMEM((pairs, LANES, 2 * LANES), F32),
                        pltpu.VMEM((pairs, 1, LANES), F32)],
        compiler_params=_params("arbitrary"), name="mixer_in",
    )(x, *layer_params, lower)


def _sb_kernel(q_ref, k_ref, v_ref, o_ref, z_ref, hl_ref, s_ref, w_ref, acc_ref, live_ref):
    tq = q_ref.shape[0]
    nsub = tq // SB_QSUB
    rows2 = 2 * SB_QSUB
    lookback = SB_WINDOW - SB_QSUB
    lane = lax.broadcasted_iota(jnp.int32, (1, LANES), 1)
    first_head = lane < HEAD_DIM

    def minus_suffix_ones(n):
        r = jnp.bitwise_and(lax.broadcasted_iota(jnp.int32, (2 * n, n), 0), n - 1)
        c = lax.broadcasted_iota(jnp.int32, (2 * n, n), 1)
        return jnp.where(r >= c, -1.0, 0.0).astype(BF16)

    def window(j):
        t0 = pl.program_id(2) * tq + j * SB_QSUB
        start = pl.multiple_of(jnp.maximum(t0 - lookback, 0), SB_QSUB)
        return t0, start

    def stacked_queries(j):
        q = q_ref[pl.ds(pl.multiple_of(j * SB_QSUB, SB_QSUB), SB_QSUB), :] * (
            1.0 / math.sqrt(HEAD_DIM))
        zero = jnp.zeros_like(q)
        return jnp.concatenate([jnp.where(first_head, q, zero), jnp.where(first_head, zero, q)],
                               axis=0)

    softplus = _softplus_log_domain

    def hi_lo(x):
        hi = x.astype(BF16)
        return jnp.concatenate([hi, (x - hi.astype(F32)).astype(BF16)], axis=1)

    def head_select(stacked):
        return jnp.where(first_head, stacked[:SB_QSUB], stacked[SB_QSUB:])

    row = lax.broadcasted_iota(jnp.int32, (rows2, SB_WINDOW), 0)
    col = lax.broadcasted_iota(jnp.int32, (rows2, SB_WINDOW), 1)
    key_minus_query = col - jnp.bitwise_and(row, SB_QSUB - 1)

    def window_pass(clamped):
        def mask(j, x):
            aligned = lookback // LANES * LANES
            if clamped or aligned == 0:
                t0, start = window(j)
                return jnp.where(key_minus_query < t0 - start, x, 0.0)
            return jnp.concatenate(
                [x[:, :aligned],
                 jnp.where(key_minus_query[:, aligned:] < lookback, x[:, aligned:], 0.0)], axis=1)

        minus_ones = minus_suffix_ones(SB_WINDOW)
        for j in range(nsub):
            _, start = window(j)
            z_ref[j] = _dot_nt(stacked_queries(j), k_ref[pl.ds(start, SB_WINDOW), :])
        for j in range(nsub):
            hl_ref[j] = hi_lo(mask(j, softplus(z_ref[j])))
        for j in range(nsub):
            s_ref[j] = _dot(hl_ref[j], minus_ones)
        any_live = None
        for j in range(nsub):
            _, start = window(j)
            suffix = s_ref[j]
            w_ref[j] = mask(j, jnp.exp(z_ref[j] + suffix)).astype(BF16)
            live = jnp.logical_and(jnp.max(suffix[:, 0:1]) > SB_LOG_WEIGHT_FLOOR, start > 0)
            live_ref[j] = live.astype(jnp.int32)
            any_live = live if any_live is None else jnp.logical_or(any_live, live)
        live_ref[nsub] = any_live.astype(jnp.int32)
        for j in range(nsub):
            _, start = window(j)
            acc_ref[j * SB_QSUB:(j + 1) * SB_QSUB, :] = head_select(
                _dot(w_ref[j], v_ref[pl.ds(start, SB_WINDOW), :]))

    first_step = pl.program_id(2) == 0
    pl.when(first_step)(functools.partial(window_pass, True))
    pl.when(jnp.logical_not(first_step))(functools.partial(window_pass, False))

    @pl.when(live_ref[nsub] != 0)
    def _():
        minus_ones = minus_suffix_ones(SB_TAIL)

        def per_group(j, _):
            @pl.when(live_ref[j] != 0)
            def _():
                _, start = window(j)
                q2 = stacked_queries(j)
                out_rows = pl.ds(pl.multiple_of(j * SB_QSUB, SB_QSUB), SB_QSUB)

                def cond(state):
                    pos, carry, _ = state
                    return jnp.logical_and(pos > 0, jnp.max(carry) > SB_LOG_WEIGHT_FLOOR)

                def body(state):
                    pos, carry, acc = state
                    pos = pos - SB_TAIL
                    keys = pl.ds(pl.multiple_of(pos, SB_TAIL), SB_TAIL)
                    z = _dot_nt(q2, k_ref[keys, :])
                    suffix = _dot(hi_lo(softplus(z)), minus_ones)
                    weights = jnp.exp(z + suffix + carry).astype(BF16)
                    return (pos, carry + suffix[:, 0:1],
                            acc + head_select(_dot(weights, v_ref[keys, :])))

                _, _, acc = lax.while_loop(
                    cond, body, (start, s_ref[j, :, 0:1], acc_ref[out_rows, :]))
                acc_ref[out_rows, :] = acc
            return 0

        lax.fori_loop(0, nsub, per_group, 0)

    o_ref[...] = acc_ref[...].astype(o_ref.dtype)


def _stick_breaking(qkv, batch):
    T, W3 = qkv.shape
    W = W3 // 3
    S = T // batch
    pairs = W // LANES
    tq = SB_QTILE
    nq = S // tq
    nsub = tq // SB_QSUB
    rows2 = 2 * SB_QSUB
    return pl.pallas_call(
        _sb_kernel, grid=(batch, pairs, nq),
        in_specs=[pl.BlockSpec((tq, LANES), lambda b, p, i: (b * nq + i, p)),
                  pl.BlockSpec((S, LANES), lambda b, p, i: (b, pairs + p)),
                  pl.BlockSpec((S, LANES), lambda b, p, i: (b, 2 * pairs + p))],
        out_specs=pl.BlockSpec((tq, LANES), lambda b, p, i: (b * nq + i, p)),
        out_shape=jax.ShapeDtypeStruct((T, W), BF16),
        scratch_shapes=[pltpu.VMEM((nsub, rows2, SB_WINDOW), F32),
                        pltpu.VMEM((nsub, rows2, 2 * SB_WINDOW), BF16),
                        pltpu.VMEM((nsub, rows2, SB_WINDOW), F32),
                        pltpu.VMEM((nsub, rows2, SB_WINDOW), BF16),
                        pltpu.VMEM((tq, LANES), F32),
                        pltpu.SMEM((nsub + 1,), jnp.int32)],
        compiler_params=_params("parallel", "parallel", "arbitrary"), name="stickbreak",
    )(qkv, qkv, qkv)


def _merge_ffn_kernel(x_ref, ya_ref, yb_ref, yc_ref, gates_ref, wa_ref, wb_ref, wc_ref, wo_ref,
                      gf_ref, wi_ref, wout_ref, gfin_ref, o_ref, act_ref, *, final_norm):
    D = x_ref.shape[1]
    FF = wout_ref.shape[0]
    merged = jnp.zeros(x_ref.shape, F32)
    for idx, (y_ref, w_ref) in enumerate(((ya_ref, wa_ref), (yb_ref, wb_ref), (yc_ref, wc_ref))):
        gate = _sigmoid(gates_ref[:, idx * D:(idx + 1) * D].astype(F32))
        merged = merged + gate * _dot(y_ref[...], w_ref[...])
    x1 = x_ref[...] + _dot(merged.astype(BF16), wo_ref[...])
    hf = _rmsnorm(x1, gf_ref[...]).astype(BF16)
    c0 = 0
    while c0 < FF:
        c1 = min(c0 + DOT_COLS, FF)
        gate = _dot(hf, wi_ref[:, c0:c1])
        up = _dot(hf, wi_ref[:, FF + c0:FF + c1])
        act_ref[:, c0:c1] = (_silu(gate) * up).astype(BF16)
        c0 = c1
    x2 = x1 + _dot(act_ref[...], wout_ref[...])
    if final_norm:
        x2 = _rmsnorm(x2, gfin_ref[...])
    o_ref[...] = x2


def _merge_ffn(x, ya, yb, yc, gates, layer_params, l, gfin, final_norm):
    T, D = x.shape
    tm = TOKEN_TILE
    tok = lambda i: (i, 0)
    ff = layer_params[6].shape[1]
    return pl.pallas_call(
        functools.partial(_merge_ffn_kernel, final_norm=final_norm), grid=(T // tm,),
        in_specs=[pl.BlockSpec((tm, D), tok), pl.BlockSpec((tm, ya.shape[1]), tok),
                  pl.BlockSpec((tm, yb.shape[1]), tok), pl.BlockSpec((tm, yc.shape[1]), tok),
                  pl.BlockSpec((tm, gates.shape[1]), tok)]
                 + [_layer(a, l) for a in layer_params] + [_full(gfin.shape)],
        out_specs=pl.BlockSpec((tm, D), tok),
        out_shape=jax.ShapeDtypeStruct((T, D), F32),
        scratch_shapes=[pltpu.VMEM((tm, ff), BF16)],
        compiler_params=_params("parallel"), name="merge_ffn",
    )(x, ya, yb, yc, gates, *layer_params, gfin)


def _block_diag(w):
    depth, G, n, _ = w.shape
    eye = jnp.eye(G, dtype=w.dtype)
    return (eye[None, :, None, :, None] * w[:, :, :, None, :]).reshape(depth, G * n, G * n)


def kernel(x, norm_mix_g, w_in, conv_lru_w, lru_wa, lru_ba, lru_wx, lru_bx, lru_lambda, conv_mlstm_w, mlstm_ig_b, mlstm_fg_b, w_out_lru, w_out_mlstm, w_out_sb, w_o, norm_ffn_g, w_ffn_in, w_ffn_out, final_norm_g):
    batch, seq, D = x.shape
    depth = w_in.shape[0]
    lru_w = conv_lru_w.shape[-1]
    mlstm_w = conv_mlstm_w.shape[-1] // 2
    heads = mlstm_ig_b.shape[-1]
    sb_w = w_out_sb.shape[1]
    T = batch * seq

    gate_lo = 2 * lru_w + 4 * mlstm_w
    gate_hi = gate_lo + 2 * heads
    assert gate_hi + 3 * sb_w + 3 * D == w_in.shape[-1]
    w_in_b = w_in.astype(BF16)
    w_in_b = jnp.concatenate(
        [w_in_b[..., :gate_hi], jnp.zeros((depth, D, LANES - 2 * heads), BF16),
         w_in_b[..., gate_hi:]], axis=-1)
    w_a, w_b, w_c, w_o_b, w_fi, w_fo = (
        w.astype(BF16) for w in (w_out_lru, w_out_mlstm, w_out_sb, w_o, w_ffn_in, w_ffn_out))
    gate_bias = jnp.pad(jnp.concatenate([mlstm_ig_b, mlstm_fg_b], axis=-1).astype(F32),
                        ((0, 0), (0, LANES - 2 * heads)))

    rows = lambda v: v.reshape(depth, 1, -1).astype(F32)
    mixer_params = (rows(norm_mix_g), w_in_b, conv_lru_w.astype(F32),
                    _block_diag(lru_wa).astype(BF16), _block_diag(lru_wx).astype(BF16),
                    rows(lru_ba), rows(lru_bx), rows(lru_lambda),
                    conv_mlstm_w.astype(F32), rows(gate_bias))
    ffn_params = (w_a, w_b, w_c, w_o_b, rows(norm_ffn_g), w_fi, w_fo)
    final_g = final_norm_g.reshape(1, D).astype(F32)

    xt = x.reshape(T, D)
    for l in range(depth):
        ya, yb, s_qkv, gates = _mixer_in(xt, mixer_params, l, sb_w, batch)
        yc = _stick_breaking(s_qkv, batch)
        xt = _merge_ffn(xt, ya, yb, yc, gates, ffn_params, l, final_g,
                        final_norm=(l == depth - 1))
    return xt.reshape(batch, seq, D)
```

```python
import functools
import math

import jax
import jax.numpy as jnp
import numpy as np
from jax import lax
from jax.experimental import pallas as pl
from jax.experimental.pallas import tpu as pltpu

F32 = jnp.float32
BF16 = jnp.bfloat16

LANES = 128
SUBLANES = 8
VMEM_LIMIT_BYTES = 56 * 1024 * 1024

EPS = 1e-6
LRU_C = 8.0
CONV_W = 4
HEAD_DIM = 64
LRU_BLOCK = 16
MLSTM_CHUNK = 64
MLSTM_CUMSUM_ROWS = 256
SB_QSUB = 64
SB_WINDOW = 256
SB_TAIL = 64
SB_LOG_WEIGHT_FLOOR = -110.0

TOKEN_TILE = 512
SB_QTILE = 1024
DOT_COLS = 512
PIECE_COLS = 256


def _params(*semantics):
    return pltpu.CompilerParams(dimension_semantics=semantics, vmem_limit_bytes=VMEM_LIMIT_BYTES)


def _full(shape):
    return pl.BlockSpec(shape, lambda *_: (0,) * len(shape))


def _layer(stacked, l):
    zeros = (0,) * (stacked.ndim - 1)
    return pl.BlockSpec((None,) + stacked.shape[1:], lambda *_: (l,) + zeros,
                        pipeline_mode=pl.Buffered(1))


def _rmsnorm(x, g):
    return x * lax.rsqrt(jnp.mean(x * x, axis=-1, keepdims=True) + EPS) * g


def _softplus(x):
    return jnp.maximum(x, 0.0) + jnp.log1p(jnp.exp(-jnp.abs(x)))


def _softplus_log_domain(x):
    return jnp.maximum(x, 0.0) + jnp.log(1.0 + jnp.exp2(jnp.abs(x) * (-1.0 / math.log(2.0))))


def _sigmoid(x):
    return 1.0 / (1.0 + jnp.exp2(x * (-1.0 / math.log(2.0))))


def _silu(x):
    return x * _sigmoid(x)


def _gelu_tanh(x):
    return 0.5 * x * (1.0 + jnp.tanh(math.sqrt(2.0 / math.pi) * (x + 0.044715 * (x * x * x))))


def _one_minus_exp(y):
    poly = jnp.full_like(y, 1.0 / math.factorial(7))
    for n in range(6, 0, -1):
        poly = poly * y + 1.0 / math.factorial(n)
    return jnp.where(y > -0.125, -(y * poly), 1.0 - jnp.exp(y))


def _dot(a, b):
    return jnp.dot(a, b, preferred_element_type=F32)


def _dot_nt(a, b):
    return lax.dot_general(a, b, (((1,), (1,)), ((), ())), preferred_element_type=F32)


def _dot_tn(a, b):
    return lax.dot_general(a, b, (((0,), (0,)), ((), ())), preferred_element_type=F32)


def _split3(x):
    h1 = x.astype(BF16)
    r1 = x - h1.astype(F32)
    h2 = r1.astype(BF16)
    return h1, h2, (r1 - h2.astype(F32)).astype(BF16)


def _dot_01_left(ones, x):
    h1, h2, h3 = _split3(x)
    return _dot(ones, h1) + _dot(ones, h2) + _dot(ones, h3)


def _causal_conv(pad_ref, x, w):
    ts = x.shape[0]
    pad_ref[0:SUBLANES, :] = pad_ref[ts:ts + SUBLANES, :]
    pad_ref[SUBLANES:, :] = x
    out = x * w[CONV_W - 1:CONV_W, :]
    for back in range(1, CONV_W):
        out = out + (pad_ref[SUBLANES - back:SUBLANES - back + ts, :]
                     * w[CONV_W - 1 - back:CONV_W - back, :])
    return out


def _lru_tile(xg_ref, cw_ref, wa_ref, wx_ref, ba_ref, bx_ref, lam_ref, o_ref, pad_ref, h_ref,
              xa_ref, r_ref, i_ref, fill):
    ts, W = o_ref.shape
    xa = _causal_conv(pad_ref, xg_ref[:, :W], cw_ref[...])
    xa_ref[...] = xa
    fill(2)
    xb = xa.astype(BF16)
    r_ref[...] = _dot(xb, wa_ref[...]) + ba_ref[...]
    i_ref[...] = _dot(xb, wx_ref[...]) + bx_ref[...]
    token = fill(2)

    log_decay = (-LRU_C) * _softplus(-lam_ref[...])
    row = lax.broadcasted_iota(jnp.int32, (LRU_BLOCK, W), 0)
    carry = h_ref[SUBLANES - 1:SUBLANES, :] + jnp.concatenate([token] * (W // LANES), axis=1)
    blocks = ts // LRU_BLOCK
    for blk in range(blocks):
        rows = slice(blk * LRU_BLOCK, (blk + 1) * LRU_BLOCK)
        log_a = log_decay * _sigmoid(r_ref[rows, :])
        a = jnp.exp(log_a)
        u = jnp.sqrt(_one_minus_exp(2.0 * log_a)) * (_sigmoid(i_ref[rows, :]) * xa_ref[rows, :])
        span = 1
        while span < LRU_BLOCK:
            keep = row >= span
            a_prev = jnp.where(keep, pltpu.roll(a, span, 0), 1.0)
            u_prev = jnp.where(keep, pltpu.roll(u, span, 0), 0.0)
            u = a * u_prev + u
            a = a * a_prev
            span *= 2
        h = u + a * carry
        carry = h[LRU_BLOCK - 1:LRU_BLOCK, :]
        o_ref[rows, :] = (h * _gelu_tanh(xg_ref[rows, W:])).astype(o_ref.dtype)
        if blk % (blocks // 4) == blocks // 4 - 1:
            token = fill(1)
            carry = carry + jnp.concatenate([token] * (W // LANES), axis=1)
    h_ref[...] = jnp.broadcast_to(carry, h_ref.shape)


def _mlstm_tile(qk_ref, vo_ref, if_ref, cw_ref, bias_ref, lower_ref, o_ref, pad_ref, c_ref, m_ref,
                fill):
    ts, W = o_ref.shape
    L = MLSTM_CHUNK
    pairs = W // LANES
    heads = W // HEAD_DIM

    qk = _silu(_causal_conv(pad_ref, qk_ref[...], cw_ref[...]))
    fill(1)
    q_all = qk[:, :W].astype(BF16)
    k_all = qk[:, W:] * (HEAD_DIM ** -0.5)
    v_all = vo_ref[:, :W]
    o_gate = _sigmoid(vo_ref[:, W:].astype(F32))

    lane = lax.broadcasted_iota(jnp.int32, (1, LANES), 1)
    first_head = lane < HEAD_DIM
    r2 = lax.broadcasted_iota(jnp.int32, (LANES, 2 * LANES), 0)
    c2 = lax.broadcasted_iota(jnp.int32, (LANES, 2 * LANES), 1)
    same_head = (r2 >= HEAD_DIM) == (jnp.bitwise_and(c2, LANES - 1) >= HEAD_DIM)
    ones_bd = jnp.where(same_head[:, :LANES], 1.0, 0.0).astype(BF16)
    rl = lax.broadcasted_iota(jnp.int32, (L, LANES), 0)
    key = jnp.bitwise_and(lax.broadcasted_iota(jnp.int32, (L, LANES), 1), HEAD_DIM - 1)
    causal2 = key <= rl
    diagonal2 = key == rl

    def stack_heads(a):
        zero = jnp.zeros_like(a)
        return jnp.concatenate([jnp.where(first_head, a, zero), jnp.where(first_head, zero, a)],
                               axis=0)

    pre = if_ref[...] + bias_ref[...]
    logf = -_softplus_log_domain(-pre)
    pos = jnp.bitwise_and(lax.broadcasted_iota(jnp.int32, (ts, LANES), 0), L - 1)
    i_rep, b_rep, m_loc = [], [], []
    for p in range(pairs):
        i_p = jnp.where(first_head, pre[:, 2 * p:2 * p + 1], pre[:, 2 * p + 1:2 * p + 2])
        f_p = jnp.where(first_head, logf[:, heads + 2 * p:heads + 2 * p + 1],
                        logf[:, heads + 2 * p + 1:heads + 2 * p + 2])
        b_p = jnp.concatenate(
            [_dot_01_left(lower_ref[...], f_p[r0:r0 + MLSTM_CUMSUM_ROWS])
             for r0 in range(0, ts, MLSTM_CUMSUM_ROWS)], axis=0)
        run = i_p - b_p
        span = 1
        while span < L:
            run = jnp.where(pos >= span, jnp.maximum(run, pltpu.roll(run, span, 0)), run)
            span *= 2
        fill(1)
        i_rep.append(i_p)
        b_rep.append(b_p)
        m_loc.append(b_p + run)

    m_state = [m_ref[p] for p in range(pairs)]
    c_state = [c_ref[p] for p in range(pairs)]

    for c in range(ts // L):
        rows = slice(c * L, (c + 1) * L)
        for p in range(pairs):
            lanes = slice(p * LANES, (p + 1) * LANES)
            q = q_all[rows, lanes]
            k = k_all[rows, lanes]
            v = v_all[rows, lanes]
            b_c = b_rep[p][rows]
            i_c = i_rep[p][rows]
            m_l = m_loc[p][rows]

            scores = _dot_nt(q, stack_heads(k.astype(BF16)))
            key_term = jnp.sum(jnp.where(diagonal2, b_c - i_c, 0.0), axis=0, keepdims=True)
            logd = (b_c - m_l) - key_term
            pw = (jnp.where(causal2, jnp.exp(logd), 0.0) * scores).astype(BF16)
            local = _dot(pw, jnp.concatenate([stack_heads(v), ones_bd], axis=1))
            b_last = b_c[L - 1:L, :]
            g = b_last - b_c + i_c
            g_max = jnp.max(g, axis=0, keepdims=True)
            kw = (k * jnp.exp(g - g_max)).astype(BF16)
            fresh = _dot_tn(kw, jnp.concatenate([v, jnp.ones_like(v)], axis=1))
            fresh = jnp.where(same_head, fresh, 0.0)

            m_prev = m_state[p]
            carried = _dot(q, c_state[p].astype(BF16))
            m_int = b_c + m_prev
            m_t = jnp.maximum(m_l, m_int)
            a_loc = jnp.exp(m_l - m_t)
            a_int = jnp.exp(m_int - m_t)
            num = a_loc * local[:, :LANES] + a_int * carried[:, :LANES]
            den = a_loc * local[:, LANES:] + a_int * carried[:, LANES:]
            h = num / jnp.maximum(jnp.abs(den), jnp.exp(-m_t))
            o_ref[rows, lanes] = (o_gate[rows, lanes] * h).astype(o_ref.dtype)

            m_new = jnp.maximum(b_last + m_prev, g_max)
            keep = jnp.exp(b_last + m_prev - m_new)
            add = jnp.exp(g_max - m_new)
            c_state[p] = (jnp.concatenate([keep, keep], axis=1) * c_state[p]
                          + jnp.concatenate([add, add], axis=1) * fresh)
            m_state[p] = m_new
            if p == 0 or c < 2:
                m_state[p] = m_new + fill(1)

    for p in range(pairs):
        c_ref[p] = c_state[p]
        m_ref[p] = m_state[p]


def _mixer_in_kernel(x_ref, g_ref, w_ref, lcw_ref, wa_ref, wx_ref, ba_ref, bx_ref, lam_ref, mcw_ref,
                     mbias_ref, lower_ref, ya_ref, yb_ref, sqkv_ref, gates_ref,
                     xg_s, qk_s, vo_s, if_s, lru_pad, lru_h, lru_xa, lru_r, lru_i, m_pad, m_c, m_m,
                     *, tiles_per_seq):
    @pl.when(lax.rem(pl.program_id(0), tiles_per_seq) == 0)
    def _():
        for ref in (lru_pad, lru_h, m_pad, m_c, m_m):
            ref[...] = jnp.zeros_like(ref)

    h = _rmsnorm(x_ref[...], g_ref[...]).astype(BF16)
    pieces = []
    first_col = 0
    for o_ref in (xg_s, qk_s, vo_s, if_s, sqkv_ref, gates_ref):
        for c0 in range(0, o_ref.shape[1], PIECE_COLS):
            c1 = min(c0 + PIECE_COLS, o_ref.shape[1])
            pieces.append((o_ref, c0, c1, first_col + c0))
        first_col += o_ref.shape[1]

    tokens = [jnp.zeros((1, LANES), F32)]

    def fill(n):
        done = pieces[:n]
        del pieces[:n]
        for o_ref, c0, c1, w0 in done:
            res = _dot(h, w_ref[:, w0:w0 + c1 - c0])
            stored = _sigmoid(res) if o_ref is gates_ref else res
            o_ref[:, c0:c1] = stored.astype(o_ref.dtype)
            tokens.append(jnp.minimum(jnp.abs(res[0:1, 0:LANES]), 0.0))
        return tokens[-1 - len(done)]

    def fill_through(ref):
        fill(max(n + 1 for n, piece in enumerate(pieces) if piece[0] is ref) if any(
            piece[0] is ref for piece in pieces) else 0)

    fill_through(xg_s)
    _lru_tile(xg_s, lcw_ref, wa_ref, wx_ref, ba_ref, bx_ref, lam_ref, ya_ref, lru_pad, lru_h,
              lru_xa, lru_r, lru_i, fill)
    fill_through(if_s)
    _mlstm_tile(qk_s, vo_s, if_s, mcw_ref, mbias_ref, lower_ref, yb_ref, m_pad, m_c, m_m, fill)
    fill(len(pieces))


def _cumsum_matrix():
    r = np.arange(MLSTM_CUMSUM_ROWS)
    lower = (r[:, None] // MLSTM_CHUNK == r[None, :] // MLSTM_CHUNK) & (r[None, :] <= r[:, None])
    return jnp.asarray(lower, BF16)


def _mixer_in(x, layer_params, l, sb_w, batch):
    T, D = x.shape
    tm = TOKEN_TILE
    w = layer_params[1]
    lru_w = layer_params[2].shape[-1]
    mlstm_w = layer_params[8].shape[-1] // 2
    pairs = mlstm_w // LANES
    lower = _cumsum_matrix()
    tok = lambda i: (i, 0)
    out_widths = (lru_w, mlstm_w, 3 * sb_w, 3 * D)
    assert w.shape[-1] == 2 * lru_w + 4 * mlstm_w + LANES + 3 * sb_w + 3 * D
    return pl.pallas_call(
        functools.partial(_mixer_in_kernel, tiles_per_seq=T // batch // tm), grid=(T // tm,),
        in_specs=[pl.BlockSpec((tm, D), tok)] + [_layer(a, l) for a in layer_params]
                 + [_full(lower.shape)],
        out_specs=[pl.BlockSpec((tm, n), tok) for n in out_widths],
        out_shape=[jax.ShapeDtypeStruct((T, n), BF16) for n in out_widths],
        scratch_shapes=[pltpu.VMEM((tm, 2 * lru_w), F32),
                        pltpu.VMEM((tm, 2 * mlstm_w), F32),
                        pltpu.VMEM((tm, 2 * mlstm_w), BF16),
                        pltpu.VMEM((tm, LANES), F32),
                        pltpu.VMEM((SUBLANES + tm, lru_w), F32),
                        pltpu.VMEM((SUBLANES, lru_w), F32),
                        pltpu.VMEM((tm, lru_w), F32),
                        pltpu.VMEM((tm, lru_w), F32),
                        pltpu.VMEM((tm, lru_w), F32),
                        pltpu.VMEM((SUBLANES + tm, 2 * mlstm_w), F32),
                        pltpu.VMEM((pairs, LANES, 2 * LANES), F32),
                        pltpu.VMEM((pairs, 1, LANES), F32)],
        compiler_params=_params("arbitrary"), name="mixer_in",
    )(x, *layer_params, lower)


def _sb_kernel(q_ref, k_ref, v_ref, o_ref, z_ref, hl_ref, s_ref, w_ref, acc_ref, live_ref):
    tq = q_ref.shape[0]
    nsub = tq // SB_QSUB
    rows2 = 2 * SB_QSUB
    lookback = SB_WINDOW - SB_QSUB
    lane = lax.broadcasted_iota(jnp.int32, (1, LANES), 1)
    first_head = lane < HEAD_DIM

    def minus_suffix_ones(n):
        r = jnp.bitwise_and(lax.broadcasted_iota(jnp.int32, (2 * n, n), 0), n - 1)
        c = lax.broadcasted_iota(jnp.int32, (2 * n, n), 1)
        return jnp.where(r >= c, -1.0, 0.0).astype(BF16)

    def window(j):
        t0 = pl.program_id(2) * tq + j * SB_QSUB
        start = pl.multiple_of(jnp.maximum(t0 - lookback, 0), SB_QSUB)
        return t0, start

    def stacked_queries(j):
        q = q_ref[pl.ds(pl.multiple_of(j * SB_QSUB, SB_QSUB), SB_QSUB), :] * (
            1.0 / math.sqrt(HEAD_DIM))
        zero = jnp.zeros_like(q)
        return jnp.concatenate([jnp.where(first_head, q, zero), jnp.where(first_head, zero, q)],
                               axis=0)

    softplus = _softplus_log_domain

    def hi_lo(x):
        hi = x.astype(BF16)
        return jnp.concatenate([hi, (x - hi.astype(F32)).astype(BF16)], axis=1)

    def head_select(stacked):
        return jnp.where(first_head, stacked[:SB_QSUB], stacked[SB_QSUB:])

    row = lax.broadcasted_iota(jnp.int32, (rows2, SB_WINDOW), 0)
    col = lax.broadcasted_iota(jnp.int32, (rows2, SB_WINDOW), 1)
    key_minus_query = col - jnp.bitwise_and(row, SB_QSUB - 1)

    def window_pass(clamped):
        def mask(j, x):
            aligned = lookback // LANES * LANES
            if clamped or aligned == 0:
                t0, start = window(j)
                return jnp.where(key_minus_query < t0 - start, x, 0.0)
            return jnp.concatenate(
                [x[:, :aligned],
                 jnp.where(key_minus_query[:, aligned:] < lookback, x[:, aligned:], 0.0)], axis=1)

        minus_ones = minus_suffix_ones(SB_WINDOW)
        for j in range(nsub):
            _, start = window(j)
            z_ref[j] = _dot_nt(stacked_queries(j), k_ref[pl.ds(start, SB_WINDOW), :])
        for j in range(nsub):
            hl_ref[j] = hi_lo(mask(j, softplus(z_ref[j])))
        for j in range(nsub):
            s_ref[j] = _dot(hl_ref[j], minus_ones)
        any_live = None
        for j in range(nsub):
            _, start = window(j)
            suffix = s_ref[j]
            w_ref[j] = mask(j, jnp.exp(z_ref[j] + suffix)).astype(BF16)
            live = jnp.logical_and(jnp.max(suffix[:, 0:1]) > SB_LOG_WEIGHT_FLOOR, start > 0)
            live_ref[j] = live.astype(jnp.int32)
            any_live = live if any_live is None else jnp.logical_or(any_live, live)
        live_ref[nsub] = any_live.astype(jnp.int32)
        for j in range(nsub):
            _, start = window(j)
            acc_ref[j * SB_QSUB:(j + 1) * SB_QSUB, :] = head_select(
                _dot(w_ref[j], v_ref[pl.ds(start, SB_WINDOW), :]))

    first_step = pl.program_id(2) == 0
    pl.when(first_step)(functools.partial(window_pass, True))
    pl.when(jnp.logical_not(first_step))(functools.partial(window_pass, False))

    @pl.when(live_ref[nsub] != 0)
    def _():
        minus_ones = minus_suffix_ones(SB_TAIL)

        def per_group(j, _):
            @pl.when(live_ref[j] != 0)
            def _():
                _, start = window(j)
                q2 = stacked_queries(j)
                out_rows = pl.ds(pl.multiple_of(j * SB_QSUB, SB_QSUB), SB_QSUB)

                def cond(state):
                    pos, carry, _ = state
                    return jnp.logical_and(pos > 0, jnp.max(carry) > SB_LOG_WEIGHT_FLOOR)

                def body(state):
                    pos, carry, acc = state
                    pos = pos - SB_TAIL
                    keys = pl.ds(pl.multiple_of(pos, SB_TAIL), SB_TAIL)
                    z = _dot_nt(q2, k_ref[keys, :])
                    suffix = _dot(hi_lo(softplus(z)), minus_ones)
                    weights = jnp.exp(z + suffix + carry).astype(BF16)
                    return (pos, carry + suffix[:, 0:1],
                            acc + head_select(_dot(weights, v_ref[keys, :])))

                _, _, acc = lax.while_loop(
                    cond, body, (start, s_ref[j, :, 0:1], acc_ref[out_rows, :]))
                acc_ref[out_rows, :] = acc
            return 0

        lax.fori_loop(0, nsub, per_group, 0)

    o_ref[...] = acc_ref[...].astype(o_ref.dtype)


def _stick_breaking(qkv, batch):
    T, W3 = qkv.shape
    W = W3 // 3
    S = T // batch
    pairs = W // LANES
    tq = SB_QTILE
    nq = S // tq
    nsub = tq // SB_QSUB
    rows2 = 2 * SB_QSUB
    return pl.pallas_call(
        _sb_kernel, grid=(batch, pairs, nq),
        in_specs=[pl.BlockSpec((tq, LANES), lambda b, p, i: (b * nq + i, p)),
                  pl.BlockSpec((S, LANES), lambda b, p, i: (b, pairs + p)),
                  pl.BlockSpec((S, LANES), lambda b, p, i: (b, 2 * pairs + p))],
        out_specs=pl.BlockSpec((tq, LANES), lambda b, p, i: (b * nq + i, p)),
        out_shape=jax.ShapeDtypeStruct((T, W), BF16),
        scratch_shapes=[pltpu.VMEM((nsub, rows2, SB_WINDOW), F32),
                        pltpu.VMEM((nsub, rows2, 2 * SB_WINDOW), BF16),
                        pltpu.VMEM((nsub, rows2, SB_WINDOW), F32),
                        pltpu.VMEM((nsub, rows2, SB_WINDOW), BF16),
                        pltpu.VMEM((tq, LANES), F32),
                        pltpu.SMEM((nsub + 1,), jnp.int32)],
        compiler_params=_params("parallel", "parallel", "arbitrary"), name="stickbreak",
    )(qkv, qkv, qkv)


def _merge_ffn_kernel(x_ref, ya_ref, yb_ref, yc_ref, gates_ref, wa_ref, wb_ref, wc_ref, wo_ref,
                      gf_ref, wi_ref, wout_ref, gfin_ref, o_ref, act_ref, *, final_norm):
    D = x_ref.shape[1]
    FF = wout_ref.shape[0]
    merged = jnp.zeros(x_ref.shape, F32)
    for idx, (y_ref, w_ref) in enumerate(((ya_ref, wa_ref), (yb_ref, wb_ref), (yc_ref, wc_ref))):
        gate = gates_ref[:, idx * D:(idx + 1) * D].astype(F32)
        merged = merged + gate * _dot(y_ref[...], w_ref[...])
    x1 = x_ref[...] + _dot(merged.astype(BF16), wo_ref[...])
    hf = _rmsnorm(x1, gf_ref[...]).astype(BF16)
    c0 = 0
    while c0 < FF:
        c1 = min(c0 + DOT_COLS, FF)
        gate = _dot(hf, wi_ref[:, c0:c1])
        up = _dot(hf, wi_ref[:, FF + c0:FF + c1])
        act_ref[:, c0:c1] = (_silu(gate) * up).astype(BF16)
        c0 = c1
    x2 = x1 + _dot(act_ref[...], wout_ref[...])
    if final_norm:
        x2 = _rmsnorm(x2, gfin_ref[...])
    o_ref[...] = x2


def _merge_ffn(x, ya, yb, yc, gates, layer_params, l, gfin, final_norm):
    T, D = x.shape
    tm = TOKEN_TILE
    tok = lambda i: (i, 0)
    ff = layer_params[6].shape[1]
    return pl.pallas_call(
        functools.partial(_merge_ffn_kernel, final_norm=final_norm), grid=(T // tm,),
        in_specs=[pl.BlockSpec((tm, D), tok), pl.BlockSpec((tm, ya.shape[1]), tok),
                  pl.BlockSpec((tm, yb.shape[1]), tok), pl.BlockSpec((tm, yc.shape[1]), tok),
                  pl.BlockSpec((tm, gates.shape[1]), tok)]
                 + [_layer(a, l) for a in layer_params] + [_full(gfin.shape)],
        out_specs=pl.BlockSpec((tm, D), tok),
        out_shape=jax.ShapeDtypeStruct((T, D), F32),
        scratch_shapes=[pltpu.VMEM((tm, ff), BF16)],
        compiler_params=_params("parallel"), name="merge_ffn",
    )(x, ya, yb, yc, gates, *layer_params, gfin)


def _block_diag(w):
    depth, G, n, _ = w.shape
    eye = jnp.eye(G, dtype=w.dtype)
    return (eye[None, :, None, :, None] * w[:, :, :, None, :]).reshape(depth, G * n, G * n)


def kernel(x, norm_mix_g, w_in, conv_lru_w, lru_wa, lru_ba, lru_wx, lru_bx, lru_lambda, conv_mlstm_w, mlstm_ig_b, mlstm_fg_b, w_out_lru, w_out_mlstm, w_out_sb, w_o, norm_ffn_g, w_ffn_in, w_ffn_out, final_norm_g):
    batch, seq, D = x.shape
    depth = w_in.shape[0]
    lru_w = conv_lru_w.shape[-1]
    mlstm_w = conv_mlstm_w.shape[-1] // 2
    heads = mlstm_ig_b.shape[-1]
    sb_w = w_out_sb.shape[1]
    T = batch * seq

    gate_lo = 2 * lru_w + 4 * mlstm_w
    gate_hi = gate_lo + 2 * heads
    assert gate_hi + 3 * sb_w + 3 * D == w_in.shape[-1]
    w_in_b = w_in.astype(BF16)
    w_in_b = jnp.concatenate(
        [w_in_b[..., :gate_hi], jnp.zeros((depth, D, LANES - 2 * heads), BF16),
         w_in_b[..., gate_hi:]], axis=-1)
    w_a, w_b, w_c, w_o_b, w_fi, w_fo = (
        w.astype(BF16) for w in (w_out_lru, w_out_mlstm, w_out_sb, w_o, w_ffn_in, w_ffn_out))
    gate_bias = jnp.pad(jnp.concatenate([mlstm_ig_b, mlstm_fg_b], axis=-1).astype(F32),
                        ((0, 0), (0, LANES - 2 * heads)))

    rows = lambda v: v.reshape(depth, 1, -1).astype(F32)
    mixer_params = (rows(norm_mix_g), w_in_b, conv_lru_w.astype(F32),
                    _block_diag(lru_wa).astype(BF16), _block_diag(lru_wx).astype(BF16),
                    rows(lru_ba), rows(lru_bx), rows(lru_lambda),
                    conv_mlstm_w.astype(F32), rows(gate_bias))
    ffn_params = (w_a, w_b, w_c, w_o_b, rows(norm_ffn_g), w_fi, w_fo)
    final_g = final_norm_g.reshape(1, D).astype(F32)

    xt = x.reshape(T, D)
    for l in range(depth):
        ya, yb, s_qkv, gates = _mixer_in(xt, mixer_params, l, sb_w, batch)
        yc = _stick_breaking(s_qkv, batch)
        xt = _merge_ffn(xt, ya, yb, yc, gates, ffn_params, l, final_g,
                        final_norm=(l == depth - 1))
    return xt.reshape(batch, seq, D)
```
